```python
import math
import jax, jax.numpy as jnp
from jax import lax
import numpy as np

D_MODEL = 1024
BATCH = 16
SEQ = 2048
DEPTH = 4

N_EVEN = (DEPTH + 1) // 2
N_ODD = DEPTH // 2
EPS = 1e-6
N_MOD = 9

HEAD_DIM = 64
MIX_A_WIDTH = D_MODEL // 2
MIX_B_WIDTH = D_MODEL - MIX_A_WIDTH
A_HEADS = MIX_A_WIDTH // (2 * HEAD_DIM)
A_VDIM = 2 * HEAD_DIM
B_Q_HEADS = MIX_B_WIDTH // HEAD_DIM
B_KV_HEADS = 2
B_GROUP = B_Q_HEADS // B_KV_HEADS
WINDOW = 128
BLOCK = 128
NUM_BUCKETS = 32
MAX_DISTANCE = 128
N_ATTN_HEADS = A_HEADS + B_Q_HEADS

C_HEADS = 8
C_KDIM = D_MODEL // C_HEADS
C_VDIM = D_MODEL // C_HEADS
CHUNK = 32

D_FF = 2816

A_QK = A_HEADS * 2 * HEAD_DIM
A_V = A_HEADS * A_VDIM
B_Q = B_Q_HEADS * HEAD_DIM
B_KV = B_KV_HEADS * HEAD_DIM
EVEN_SPLITS = (A_QK, 2 * A_QK, 2 * A_QK + A_V, 2 * A_QK + A_V + B_Q, 2 * A_QK + A_V + B_Q + B_KV)
EVEN_IN = 2 * A_QK + A_V + B_Q + 2 * B_KV
EVEN_OUT = A_V + B_Q
C_HK = C_HEADS * C_KDIM
C_HV = C_HEADS * C_VDIM
ODD_SPLITS = (C_HK, 2 * C_HK, 3 * C_HK, 3 * C_HK + C_HV)
ODD_IN = 3 * C_HK + 2 * C_HV

kernel_name = "hybrid_diffattn_swa_hgrn2_macaron_encoder"


def rms_norm(x, g):
    xf = x.astype(jnp.float32)
    y = xf * lax.rsqrt(jnp.mean(xf * xf, axis=-1, keepdims=True) + EPS)
    return (y * g.astype(jnp.float32)).astype(x.dtype)


def modulate(h, shift, scale):
    return h * (1 + scale[:, None, :]) + shift[:, None, :]


def swiglu(h, w_up, w_down):
    a, b = jnp.split(h @ w_up, 2, axis=-1)
    return (jax.nn.silu(a) * b) @ w_down


def t5_bucket(rel):
    half = NUM_BUCKETS // 2
    max_exact = half // 2
    n = jnp.abs(rel)
    nf = jnp.maximum(n, 1).astype(jnp.float32)
    large = max_exact + (jnp.log(nf / max_exact) / math.log(MAX_DISTANCE / max_exact)
                         * (half - max_exact)).astype(jnp.int32)
    large = jnp.minimum(large, half - 1)
    return jnp.where(rel > 0, half, 0) + jnp.where(n < max_exact, n, large)


def diff_attention(q1, q2, k1, k2, v, lam, lam_init, table_a, subln_g):
    bsz, s = q1.shape[:2]
    nb = s // BLOCK
    scale = HEAD_DIM ** -0.5
    kpos = jnp.arange(s)

    def to_blocks(t):
        return jnp.moveaxis(t.reshape(bsz, nb, BLOCK, A_HEADS, HEAD_DIM), 1, 0)

    def block_fn(args):
        qb1, qb2, idx = args
        qpos = idx * BLOCK + jnp.arange(BLOCK)
        bias = jnp.transpose(table_a[t5_bucket(kpos[None, :] - qpos[:, None])], (2, 0, 1))
        bias = bias.astype(jnp.float32)
        s1 = jnp.einsum('bqhd,bkhd->bhqk', qb1, k1).astype(jnp.float32) * scale + bias
        s2 = jnp.einsum('bqhd,bkhd->bhqk', qb2, k2).astype(jnp.float32) * scale + bias
        p = jax.nn.softmax(s1, axis=-1) - lam * jax.nn.softmax(s2, axis=-1)
        return jnp.einsum('bhqk,bkhv->bqhv', p.astype(v.dtype), v)

    o = lax.map(block_fn, (to_blocks(q1), to_blocks(q2), jnp.arange(nb)))
    o = jnp.moveaxis(o, 0, 1).reshape(bsz, s, A_HEADS, A_VDIM)
    o = rms_norm(o, subln_g) * (1 - lam_init)
    return o.reshape(bsz, s, A_HEADS * A_VDIM)


def window_attention(q, k, v, sink, table_b):
    bsz, s = q.shape[:2]
    nb = s // BLOCK
    scale = HEAD_DIM ** -0.5
    qb = q.reshape(bsz, nb, BLOCK, B_KV_HEADS, B_GROUP, HEAD_DIM)

    def neighbours(t):
        tp = jnp.pad(t, ((0, 0), (BLOCK, BLOCK), (0, 0), (0, 0)))
        tp = tp.reshape(bsz, nb + 2, BLOCK, B_KV_HEADS, HEAD_DIM)
        return jnp.concatenate([tp[:, :-2], tp[:, 1:-1], tp[:, 2:]], axis=2)

    kn, vn = neighbours(k), neighbours(v)
    rel = jnp.arange(3 * BLOCK)[None, :] - BLOCK - jnp.arange(BLOCK)[:, None]
    kpos = (jnp.arange(nb)[:, None] - 1) * BLOCK + jnp.arange(3 * BLOCK)[None, :]
    valid = (jnp.abs(rel) <= WINDOW)[None] & ((kpos >= 0) & (kpos < s))[:, None, :]
    bias = jnp.transpose(table_b[t5_bucket(rel)], (2, 0, 1))
    bias = bias.reshape(B_KV_HEADS, B_GROUP, BLOCK, 3 * BLOCK).astype(jnp.float32)
    sc = jnp.einsum('bnqhgd,bnkhd->bnhgqk', qb, kn).astype(jnp.float32) * scale + bias
    sc = jnp.where(valid[None, :, None, None], sc, -jnp.inf)
    sk = sink.astype(jnp.float32).reshape(1, 1, B_KV_HEADS, B_GROUP, 1, 1)
    m = jnp.maximum(jnp.max(sc, axis=-1, keepdims=True), sk)
    e = jnp.exp(sc - m)
    p = e / (jnp.sum(e, axis=-1, keepdims=True) + jnp.exp(sk - m))
    o = jnp.einsum('bnhgqk,bnkhd->bnqhgd', p.astype(v.dtype), vn)
    return o.reshape(bsz, s, B_Q_HEADS * HEAD_DIM)


def even_mixer(h, w_in, w_out, qk_g, lam_p, subln_g, sink, rel_bias, layer_idx):
    bsz, s, _ = h.shape
    aq, ak, av, bq, bk, bv = jnp.split(h @ w_in, EVEN_SPLITS, axis=-1)
    aq = rms_norm(aq.reshape(bsz, s, A_HEADS, 2, HEAD_DIM), qk_g[0])
    ak = rms_norm(ak.reshape(bsz, s, A_HEADS, 2, HEAD_DIM), qk_g[1])
    av = av.reshape(bsz, s, A_HEADS, A_VDIM)
    lam_init = 0.8 - 0.6 * math.exp(-0.3 * layer_idx)
    lp = lam_p.astype(jnp.float32)
    lam = jnp.exp(jnp.sum(lp[0] * lp[1])) - jnp.exp(jnp.sum(lp[2] * lp[3])) + lam_init
    ya = diff_attention(aq[..., 0, :], aq[..., 1, :], ak[..., 0, :], ak[..., 1, :], av,
                        lam, lam_init, rel_bias[:, :A_HEADS], subln_g)
    bq = rms_norm(bq.reshape(bsz, s, B_Q_HEADS, HEAD_DIM), qk_g[2])
    bk = rms_norm(bk.reshape(bsz, s, B_KV_HEADS, HEAD_DIM), qk_g[3])
    bv = bv.reshape(bsz, s, B_KV_HEADS, HEAD_DIM)
    yb = window_attention(bq, bk, bv, sink, rel_bias[:, A_HEADS:])
    return jnp.concatenate([ya, yb], axis=-1) @ w_out


def chunk_gla(q, k, v, log_f):
    n, s, h, dk = q.shape
    dv = v.shape[-1]
    nc = s // CHUNK

    def chunks(t):
        return jnp.moveaxis(t.reshape(n, nc, CHUNK, h, t.shape[-1]), 1, 0)

    g_cum = jnp.cumsum(chunks(log_f), axis=2)
    causal = jnp.tril(jnp.ones((CHUNK, CHUNK), dtype=bool))[None, :, :, None, None]

    def step(state, inp):
        qc, kc, vc, gc = inp
        decay = jnp.exp(jnp.where(causal, gc[:, :, None] - gc[:, None], -jnp.inf))
        attn = jnp.einsum('ntshd,nshd->nhts', qc[:, :, None] * decay, kc)
        o = jnp.einsum('nhts,nshv->nthv', attn, vc) + jnp.einsum('nthd,nhdv->nthv', qc * jnp.exp(gc), state)
        g_last = gc[:, -1]
        state = jnp.exp(g_last)[..., None] * state + jnp.einsum(
            'nshd,nshv->nhdv', kc * jnp.exp(g_last[:, None] - gc), vc)
        return state, o

    s0 = jnp.zeros((n, h, dk, dv), jnp.float32)
    _, o = lax.scan(step, s0, (chunks(q), chunks(k), chunks(v), g_cum))
    return jnp.moveaxis(o, 0, 1).reshape(n, s, h, dv)


def log_forget(f_raw, lb):
    lb = lb.reshape(C_HEADS, C_KDIM)
    return jnp.logaddexp(jnp.log(lb), jnp.log1p(-lb) + jax.nn.log_sigmoid(f_raw.astype(jnp.float32)))


def hgrn2_bidirectional(q, i_in, log_f_fwd, log_f_bwd):
    bsz = q.shape[0]
    flip = lambda t: jnp.flip(t, axis=1)
    qq = jnp.concatenate([q, flip(q)], axis=0).astype(jnp.float32)
    vv = jnp.concatenate([i_in, flip(i_in)], axis=0).astype(jnp.float32)
    lf = jnp.concatenate([log_f_fwd, flip(log_f_bwd)], axis=0)
    kk = -jnp.expm1(lf)
    o = chunk_gla(qq, kk, vv, lf)
    return o[:bsz] + flip(o[bsz:])


def odd_mixer(h, w_in, w_out, lb, out_g):
    bsz, s, _ = h.shape
    q, ff, fb, iv, g = jnp.split(h @ w_in, ODD_SPLITS, axis=-1)
    heads_k = lambda t: t.reshape(bsz, s, C_HEADS, C_KDIM)
    heads_v = lambda t: t.reshape(bsz, s, C_HEADS, C_VDIM)
    o = hgrn2_bidirectional(jax.nn.silu(heads_k(q)), heads_v(iv),
                            log_forget(heads_k(ff), lb[0]), log_forget(heads_k(fb), lb[1]))
    o = rms_norm(o.astype(h.dtype), out_g) * jax.nn.silu(heads_v(g))
    return o.reshape(bsz, s, C_HV) @ w_out


def setup_inputs(seed: int = 0) -> dict:
    key = jax.random.key(seed)
    ks = jax.random.split(key, 18)
    nrm = lambda k, shape, sc: jax.random.normal(k, shape, jnp.float32) * sc
    return {
        "x": nrm(ks[0], (BATCH, SEQ, D_MODEL), 1.0),
        "c": nrm(ks[1], (BATCH, D_MODEL), 1.0),
        "ada_w": nrm(ks[2], (DEPTH, D_MODEL, N_MOD * D_MODEL), 0.5 * D_MODEL ** -0.5),
        "ada_b": nrm(ks[3], (DEPTH, N_MOD * D_MODEL), 0.02),
        "norm_g": 1.0 + nrm(ks[4], (DEPTH, 3, D_MODEL), 0.02),
        "ffn_up": nrm(ks[5], (DEPTH, 2, D_MODEL, 2 * D_FF), D_MODEL ** -0.5),
        "ffn_down": nrm(ks[6], (DEPTH, 2, D_FF, D_MODEL), D_FF ** -0.5),
        "even_w_in": nrm(ks[7], (N_EVEN, D_MODEL, EVEN_IN), D_MODEL ** -0.5),
        "even_w_out": nrm(ks[8], (N_EVEN, EVEN_OUT, D_MODEL), EVEN_OUT ** -0.5),
        "qk_norm_g": 1.0 + nrm(ks[9], (N_EVEN, 4, HEAD_DIM), 0.02),
        "diff_lambda": nrm(ks[10], (N_EVEN, 4, HEAD_DIM), 0.1),
        "diff_subln_g": 1.0 + nrm(ks[11], (N_EVEN, A_VDIM), 0.02),
        "sink_logit": nrm(ks[12], (N_EVEN, B_Q_HEADS), 0.5),
        "rel_bias": nrm(ks[13], (NUM_BUCKETS, N_ATTN_HEADS), 0.1),
        "odd_w_in": nrm(ks[14], (N_ODD, D_MODEL, ODD_IN), D_MODEL ** -0.5),
        "odd_w_out": nrm(ks[15], (N_ODD, C_HV, D_MODEL), C_HV ** -0.5),
        "c_lower_bound": nrm(ks[16], (2, DEPTH, C_HK), 0.1),
        "c_out_norm_g": 1.0 + nrm(ks[17], (N_ODD, C_VDIM), 0.02),
    }


def reference(x, c, ada_w, ada_b, norm_g, ffn_up, ffn_down, even_w_in, even_w_out, qk_norm_g,
              diff_lambda, diff_subln_g, sink_logit, rel_bias, odd_w_in, odd_w_out,
              c_lower_bound, c_out_norm_g):
    mod_all = jnp.einsum('bd,ldm->lbm', jax.nn.silu(c), ada_w) + ada_b[:, None, :]
    sm = jax.nn.softmax(c_lower_bound.astype(jnp.float32), axis=1)
    lb_all = jnp.cumsum(sm, axis=1) - sm[:, :1]
    for l in range(DEPTH):
        sh1, sc1, g1, sh2, sc2, g2, sh3, sc3, g3 = jnp.split(mod_all[l], N_MOD, axis=-1)
        h = modulate(rms_norm(x, norm_g[l, 0]), sh1, sc1)
        x = x + 0.5 * g1[:, None, :] * swiglu(h, ffn_up[l, 0], ffn_down[l, 0])
        h = modulate(rms_norm(x, norm_g[l, 1]), sh2, sc2)
        if l % 2 == 0:
            e = l // 2
            y = even_mixer(h, even_w_in[e], even_w_out[e], qk_norm_g[e], diff_lambda[e],
                           diff_subln_g[e], sink_logit[e], rel_bias, l)
        else:
            o = l // 2
            y = odd_mixer(h, odd_w_in[o], odd_w_out[o], lb_all[:, l], c_out_norm_g[o])
        x = x + g2[:, None, :] * y
        h = modulate(rms_norm(x, norm_g[l, 2]), sh3, sc3)
        x = x + 0.5 * g3[:, None, :] * swiglu(h, ffn_up[l, 1], ffn_down[l, 1])
    return x
```

```python
import functools
import math

import jax
import jax.numpy as jnp
from jax import lax
from jax.experimental import pallas as pl
from jax.experimental.pallas import tpu as pltpu

F32 = jnp.float32
BF16 = jnp.bfloat16

EPS = 1e-6
N_MOD = 9
HEAD_DIM = 64
LANES = 128
A_HEADS = 4
B_Q_HEADS = 8
B_KV_HEADS = 2
WINDOW = 128
NUM_BUCKETS = 32
MAX_DISTANCE = 128
C_HEADS = 8
GLA_CHUNK = 128
GLA_DIAG = 8
V7X_VMEM_LIMIT_BYTES = 56 * 1024 * 1024

_TRANS_B = (((1,), (1,)), ((), ()))
_TRANS_A = (((0,), (0,)), ((), ()))


def _params(*sem):
    return pltpu.CompilerParams(dimension_semantics=sem, vmem_limit_bytes=V7X_VMEM_LIMIT_BYTES)


def _dot(a, b):
    return jnp.dot(a, b, preferred_element_type=F32)


def _silu(t):
    return t / (1.0 + jnp.exp(-t))


def _norm_mod(x, g, shift, scale):
    ms = jnp.mean(x * x, axis=-1, keepdims=True)
    y = x * lax.rsqrt(ms + EPS) * g
    return y * (1.0 + scale) + shift


def _norm_halves(x, g, first):
    sq = x * x
    s0 = jnp.sum(jnp.where(first, sq, 0.0), axis=-1, keepdims=True)
    s1 = jnp.sum(jnp.where(first, 0.0, sq), axis=-1, keepdims=True)
    ms = jnp.where(first, s0, s1) * (1.0 / HEAD_DIM)
    return x * lax.rsqrt(ms + EPS) * g


def _mod_kernel(c_ref, w_ref, b_ref, o_ref):
    sc = _silu(c_ref[...])
    o_ref[0] = jnp.dot(sc, w_ref[0], precision=lax.Precision.HIGHEST,
                       preferred_element_type=F32) + b_ref[0]


def _mod_call(c, ada_w, ada_b):
    depth, d, m = ada_w.shape
    bsz = c.shape[0]
    tn = m // 8
    return pl.pallas_call(
        _mod_kernel,
        grid=(depth, m // tn),
        in_specs=[pl.BlockSpec((bsz, d), lambda l, j: (0, 0)),
                  pl.BlockSpec((1, d, tn), lambda l, j: (l, 0, j)),
                  pl.BlockSpec((1, 1, tn), lambda l, j: (l, 0, j))],
        out_specs=pl.BlockSpec((1, bsz, tn), lambda l, j: (l, 0, j)),
        out_shape=jax.ShapeDtypeStruct((depth, bsz, m), F32),
        compiler_params=_params("arbitrary", "arbitrary"),
        name="adaln_mod",
    )(c, ada_w, ada_b.reshape(depth, 1, m))


def _ffn_kernel(x_ref, g_ref, sh_ref, sc_ref, gate_ref, wa_ref, wb_ref, wd_ref, o_ref, h_scr, acc_scr):
    j = pl.program_id(2)

    @pl.when(j == 0)
    def _():
        h_scr[...] = _norm_mod(x_ref[0], g_ref[...], sh_ref[0], sc_ref[0]).astype(BF16)
        acc_scr[...] = jnp.zeros_like(acc_scr)

    h = h_scr[...]
    a = _dot(h, wa_ref[...])
    b = _dot(h, wb_ref[...])
    act = (_silu(a) * b).astype(BF16)
    acc_scr[...] += _dot(act, wd_ref[...])

    @pl.when(j == pl.num_programs(2) - 1)
    def _():
        o_ref[0] = x_ref[0] + 0.5 * gate_ref[0] * acc_scr[...]


def _ffn_call(x, g, shift, scale, gate, w_up, w_down):
    bsz, s, d = x.shape
    f = w_down.shape[0]
    tm = min(512, s)
    nf = 2
    fc = f // nf
    row = lambda b, i, j: (b, i, 0)
    vec = lambda b, i, j: (b, 0, 0)
    return pl.pallas_call(
        _ffn_kernel,
        grid=(bsz, s // tm, nf),
        in_specs=[pl.BlockSpec((1, tm, d), row),
                  pl.BlockSpec((1, d), lambda b, i, j: (0, 0)),
                  pl.BlockSpec((1, 1, d), vec),
                  pl.BlockSpec((1, 1, d), vec),
                  pl.BlockSpec((1, 1, d), vec),
                  pl.BlockSpec((d, fc), lambda b, i, j: (0, j)),
                  pl.BlockSpec((d, fc), lambda b, i, j: (0, nf + j)),
                  pl.BlockSpec((fc, d), lambda b, i, j: (j, 0))],
        out_specs=pl.BlockSpec((1, tm, d), row),
        out_shape=jax.ShapeDtypeStruct(x.shape, F32),
        scratch_shapes=[pltpu.VMEM((tm, d), BF16), pltpu.VMEM((tm, d), F32)],
        compiler_params=_params("arbitrary", "arbitrary", "arbitrary"),
        name="ffn",
    )(x, g.reshape(1, d), shift, scale, gate, w_up, w_up, w_down)


def _even_in_kernel(x_ref, g_ref, sh_ref, sc_ref, w_ref, o_ref):
    h = _norm_mod(x_ref[0], g_ref[...], sh_ref[0], sc_ref[0]).astype(BF16)
    o_ref[0] = _dot(h, w_ref[...]).astype(BF16)


def _even_in_call(x, g, shift, scale, w):
    bsz, s, d = x.shape
    n = w.shape[1]
    tm = min(512, s)
    row = lambda b, i: (b, i, 0)
    vec = lambda b, i: (b, 0, 0)
    return pl.pallas_call(
        _even_in_kernel,
        grid=(bsz, s // tm),
        in_specs=[pl.BlockSpec((1, tm, d), row),
                  pl.BlockSpec((1, d), lambda b, i: (0, 0)),
                  pl.BlockSpec((1, 1, d), vec),
                  pl.BlockSpec((1, 1, d), vec),
                  pl.BlockSpec((d, n), lambda b, i: (0, 0))],
        out_specs=pl.BlockSpec((1, tm, n), row),
        out_shape=jax.ShapeDtypeStruct((bsz, s, n), BF16),
        compiler_params=_params("arbitrary", "arbitrary"),
        name="even_in_proj",
    )(x, g.reshape(1, d), shift, scale, w)


def _log_forget_and_input_gate(f_raw, clb, layer):
    e = jnp.exp(clb - jnp.max(clb, axis=0, keepdims=True))
    lb = jnp.sum(e[1:layer + 1], axis=0, keepdims=True) / jnp.sum(e, axis=0, keepdims=True)
    log_lb = jnp.log(lb)
    log_1m = jnp.log(1.0 - lb)
    log_sig = jnp.minimum(f_raw, 0.0) - jnp.log(1.0 + jnp.exp(-jnp.abs(f_raw)))
    b = log_1m + log_sig
    lf = jnp.maximum(log_lb, b) + jnp.log(1.0 + jnp.exp(-jnp.abs(log_lb - b)))
    return lf, 1.0 - jnp.exp(lf)


def _odd_in_kernel(x_ref, g_ref, sh_ref, sc_ref, w_ref, clb_ref,
                   qs_ref, lff_ref, lfb_ref, kf_ref, kb_ref, v_ref, gs_ref, h_scr, *, layer):
    j = pl.program_id(2)

    @pl.when(j == 0)
    def _():
        h_scr[...] = _norm_mod(x_ref[0], g_ref[...], sh_ref[0], sc_ref[0]).astype(BF16)

    y = _dot(h_scr[...], w_ref[...])

    @pl.when(j == 0)
    def _():
        qs_ref[0] = _silu(y).astype(BF16)

    @pl.when(j == 1)
    def _():
        lf, k = _log_forget_and_input_gate(y, clb_ref[0], layer)
        lff_ref[0] = lf
        kf_ref[0] = k.astype(BF16)

    @pl.when(j == 2)
    def _():
        lf, k = _log_forget_and_input_gate(y, clb_ref[1], layer)
        lfb_ref[0] = lf
        kb_ref[0] = k.astype(BF16)

    @pl.when(j == 3)
    def _():
        v_ref[0] = y.astype(BF16)

    @pl.when(j == 4)
    def _():
        gs_ref[0] = _silu(y).astype(BF16)


def _odd_in_call(x, g, shift, scale, w, clb, layer):
    bsz, s, d = x.shape
    hk = clb.shape[-1]
    tm = min(512, s)
    row = lambda b, i, j: (b, i, 0)
    vec = lambda b, i, j: (b, 0, 0)
    out = lambda dt: jax.ShapeDtypeStruct((bsz, s, hk), dt)
    return pl.pallas_call(
        functools.partial(_odd_in_kernel, layer=layer),
        grid=(bsz, s // tm, 5),
        in_specs=[pl.BlockSpec((1, tm, d), row),
                  pl.BlockSpec((1, d), lambda b, i, j: (0, 0)),
                  pl.BlockSpec((1, 1, d), vec),
                  pl.BlockSpec((1, 1, d), vec),
                  pl.BlockSpec((d, hk), lambda b, i, j: (0, j)),
                  pl.BlockSpec(clb.shape, lambda b, i, j: (0, 0, 0))],
        out_specs=[pl.BlockSpec((1, tm, hk), row)] * 7,
        out_shape=[out(BF16), out(F32), out(F32), out(BF16), out(BF16), out(BF16), out(BF16)],
        scratch_shapes=[pltpu.VMEM((tm, d), BF16)],
        compiler_params=_params("arbitrary", "arbitrary", "arbitrary"),
        name="odd_in_proj",
    )(x, g.reshape(1, d), shift, scale, w, clb)


def _out_kernel(*refs, n_in):
    x_ref, gate_ref = refs[0], refs[1]
    y_refs = refs[2:2 + n_in]
    w_refs = refs[2 + n_in:2 + 2 * n_in]
    o_ref = refs[2 + 2 * n_in]
    acc = _dot(y_refs[0][0], w_refs[0][...])
    for y_ref, w_ref in zip(y_refs[1:], w_refs[1:]):
        acc += _dot(y_ref[0], w_ref[...])
    o_ref[0] = x_ref[0] + gate_ref[0] * acc


def _out_call(x, gate, ys, w):
    bsz, s, d = x.shape
    tm = min(512, s)
    n_in = len(ys)
    wk = ys[0].shape[-1]
    row = lambda b, i: (b, i, 0)
    return pl.pallas_call(
        functools.partial(_out_kernel, n_in=n_in),
        grid=(bsz, s // tm),
        in_specs=([pl.BlockSpec((1, tm, d), row), pl.BlockSpec((1, 1, d), lambda b, i: (b, 0, 0))]
                  + [pl.BlockSpec((1, tm, wk), row)] * n_in
                  + [pl.BlockSpec((wk, d), functools.partial(lambda b, i, k: (k, 0), k=k)) for k in range(n_in)]),
        out_specs=pl.BlockSpec((1, tm, d), row),
        out_shape=jax.ShapeDtypeStruct(x.shape, F32),
        compiler_params=_params("arbitrary", "arbitrary"),
        name="out_proj",
    )(x, gate, *ys, *([w] * n_in))


def _t5_bucket(rel):
    half = NUM_BUCKETS // 2
    max_exact = half // 2
    n = jnp.abs(rel)
    nf = jnp.maximum(n, 1).astype(F32)
    large = max_exact + (jnp.log(nf / max_exact) / math.log(MAX_DISTANCE / max_exact)
                         * (half - max_exact)).astype(jnp.int32)
    large = jnp.minimum(large, half - 1)
    return jnp.where(rel > 0, half, 0) + jnp.where(n < max_exact, n, large)


def _bias_tiles(rel_bias):
    r = jnp.arange(LANES)[:, None]
    c = jnp.arange(LANES)[None, :]
    rel_a = (jnp.arange(-2, 3) * LANES)[:, None, None] + (c - r)[None]
    tiles_a = jnp.transpose(rel_bias[:, :A_HEADS][_t5_bucket(rel_a)], (3, 0, 1, 2)).astype(F32)
    rel_b = jnp.arange(3 * LANES)[None, :] - LANES - r
    tiles_b = jnp.transpose(rel_bias[:, A_HEADS:][_t5_bucket(rel_b)], (2, 0, 1)).astype(F32)
    return tiles_a, tiles_b


def _diff_attn_kernel(q_ref, k_ref, v_ref, bias_ref, qg_ref, kg_ref, lam_ref, sg_ref, o_ref,
                      kn_scr, bias_scr, *, lam_init):
    i = pl.program_id(2)
    tq = q_ref.shape[1]
    s_len = k_ref.shape[1]
    first = lax.broadcasted_iota(jnp.int32, (1, LANES), 1) < HEAD_DIM

    @pl.when(i == 0)
    def _():
        kn_scr[...] = _norm_halves(k_ref[0].astype(F32), kg_ref[...], first).astype(BF16)

    for a in range(tq // LANES):
        qb = i * (tq // LANES) + a
        for j in range(s_len // LANES):
            idx = jnp.clip(j - qb, -2, 2) + 2
            bias_scr[a * LANES:(a + 1) * LANES, j * LANES:(j + 1) * LANES] = bias_ref[0, idx]

    lp = lam_ref[...]
    lam = (jnp.exp(jnp.sum(lp[0:1] * lp[1:2], axis=-1, keepdims=True))
           - jnp.exp(jnp.sum(lp[2:3] * lp[3:4], axis=-1, keepdims=True)) + lam_init)

    qn = _norm_halves(q_ref[0].astype(F32), qg_ref[...], first) * (HEAD_DIM ** -0.5)
    kn = kn_scr[...]
    bias = bias_scr[...]

    def softmax_rows(qm):
        sc = lax.dot_general(qm.astype(BF16), kn, _TRANS_B, preferred_element_type=F32) + bias
        e = jnp.exp(sc - jnp.max(sc, axis=-1, keepdims=True))
        return e * (1.0 / jnp.sum(e, axis=-1, keepdims=True))

    p = softmax_rows(jnp.where(first, qn, 0.0)) - lam * softmax_rows(jnp.where(first, 0.0, qn))
    o = _dot(p.astype(BF16), v_ref[0])
    o = o * lax.rsqrt(jnp.mean(o * o, axis=-1, keepdims=True) + EPS) * sg_ref[...] * (1.0 - lam_init)
    o_ref[0] = o.astype(BF16)


def _diff_attn_call(qkv, tiles_a, qk_g, lam_p, subln_g, lam_init):
    bsz, s, _ = qkv.shape
    tq = min(256, s)
    dup = lambda g: jnp.concatenate([g, g]).reshape(1, LANES)
    return pl.pallas_call(
        functools.partial(_diff_attn_kernel, lam_init=lam_init),
        grid=(bsz, A_HEADS, s // tq),
        in_specs=[pl.BlockSpec((1, tq, LANES), lambda b, h, i: (b, i, h)),
                  pl.BlockSpec((1, s, LANES), lambda b, h, i: (b, 0, A_HEADS + h)),
                  pl.BlockSpec((1, s, LANES), lambda b, h, i: (b, 0, 2 * A_HEADS + h)),
                  pl.BlockSpec((1, 5, LANES, LANES), lambda b, h, i: (h, 0, 0, 0)),
                  pl.BlockSpec((1, LANES), lambda b, h, i: (0, 0)),
                  pl.BlockSpec((1, LANES), lambda b, h, i: (0, 0)),
                  pl.BlockSpec(lam_p.shape, lambda b, h, i: (0, 0)),
                  pl.BlockSpec((1, LANES), lambda b, h, i: (0, 0))],
        out_specs=pl.BlockSpec((1, tq, LANES), lambda b, h, i: (b, i, h)),
        out_shape=jax.ShapeDtypeStruct((bsz, s, A_HEADS * LANES), BF16),
        scratch_shapes=[pltpu.VMEM((s, LANES), BF16), pltpu.VMEM((tq, s), F32)],
        compiler_params=_params("arbitrary", "arbitrary", "arbitrary"),
        name="diff_attn",
    )(qkv, qkv, qkv, tiles_a, dup(qk_g[0]), dup(qk_g[1]), lam_p, subln_g.reshape(1, LANES))


def _win_attn_kernel(q_ref, k_ref, v_ref, bias_ref, qg_ref, kg_ref, sink_ref, o_ref, kn_scr):
    i = pl.program_id(1)
    nb = pl.num_programs(1)
    first = lax.broadcasted_iota(jnp.int32, (1, LANES), 1) < HEAD_DIM

    @pl.when(i == 0)
    def _():
        kn_scr[...] = _norm_halves(k_ref[0].astype(F32), kg_ref[...], first).astype(BF16)

    blocks = (jnp.maximum(i - 1, 0), i, jnp.minimum(i + 1, nb - 1))
    starts = [pl.multiple_of(j * LANES, LANES) for j in blocks]
    k3 = jnp.concatenate([kn_scr[pl.ds(st, LANES), :] for st in starts], axis=0)
    v3 = jnp.concatenate([v_ref[0, pl.ds(st, LANES), :] for st in starts], axis=0)

    r = lax.broadcasted_iota(jnp.int32, (LANES, 1), 0)
    c = lax.broadcasted_iota(jnp.int32, (1, 3 * LANES), 1)
    valid = ((jnp.abs(c - LANES - r) <= WINDOW)
             & ((c >= LANES) | (i > 0)) & ((c < 2 * LANES) | (i < nb - 1)))

    group = B_Q_HEADS // B_KV_HEADS
    for pb in range(B_Q_HEADS // 2):
        qp = _norm_halves(q_ref[0, :, pb * LANES:(pb + 1) * LANES].astype(F32), qg_ref[...], first)
        qp = qp * (HEAD_DIM ** -0.5)
        qp_swapped = pltpu.roll(qp, HEAD_DIM, 1)
        kv = (2 * pb) // group
        kv_lanes = first if kv == 0 else jnp.logical_not(first)
        halves = []
        for a in range(2):
            hq = 2 * pb + a
            src = qp if a == kv else qp_swapped
            qm = jnp.where(kv_lanes, src, 0.0).astype(BF16)
            sc = lax.dot_general(qm, k3, _TRANS_B, preferred_element_type=F32) + bias_ref[hq]
            sc = jnp.where(valid, sc, -jnp.inf)
            sk = sink_ref[hq:hq + 1, 0:1]
            m = jnp.maximum(jnp.max(sc, axis=-1, keepdims=True), sk)
            e = jnp.exp(sc - m)
            p = e / (jnp.sum(e, axis=-1, keepdims=True) + jnp.exp(sk - m))
            o = _dot(p.astype(BF16), v3)
            halves.append(o if a == kv else pltpu.roll(o, HEAD_DIM, 1))
        o_ref[0, :, pb * LANES:(pb + 1) * LANES] = jnp.where(first, halves[0], halves[1]).astype(BF16)


def _win_attn_call(qkv, tiles_b, qk_g, sink):
    bsz, s, n = qkv.shape
    qw = B_Q_HEADS * HEAD_DIM
    q_blk = (2 * A_HEADS * LANES + A_HEADS * LANES) // qw
    k_blk = (2 * A_HEADS * LANES + A_HEADS * LANES + qw) // LANES
    dup = lambda g: jnp.concatenate([g, g]).reshape(1, LANES)
    return pl.pallas_call(
        _win_attn_kernel,
        grid=(bsz, s // LANES),
        in_specs=[pl.BlockSpec((1, LANES, qw), lambda b, i: (b, i, q_blk)),
                  pl.BlockSpec((1, s, LANES), lambda b, i: (b, 0, k_blk)),
                  pl.BlockSpec((1, s, LANES), lambda b, i: (b, 0, k_blk + 1)),
                  pl.BlockSpec(tiles_b.shape, lambda b, i: (0, 0, 0)),
                  pl.BlockSpec((1, LANES), lambda b, i: (0, 0)),
                  pl.BlockSpec((1, LANES), lambda b, i: (0, 0)),
                  pl.BlockSpec((B_Q_HEADS, LANES), lambda b, i: (0, 0))],
        out_specs=pl.BlockSpec((1, LANES, qw), lambda b, i: (b, i, 0)),
        out_shape=jax.ShapeDtypeStruct((bsz, s, qw), BF16),
        scratch_shapes=[pltpu.VMEM((s, LANES), BF16)],
        compiler_params=_params("arbitrary", "arbitrary"),
        name="win_attn",
    )(qkv, qkv, qkv, tiles_b, dup(qk_g[2]), dup(qk_g[3]),
      jnp.broadcast_to(sink.astype(F32)[:, None], (B_Q_HEADS, LANES)))


def _gla_chunk(q, k, v, lf, st, tri, row, col, rev):
    cs = q.shape[0]
    lf_hi = lf.astype(BF16)
    lf_lo = (lf - lf_hi.astype(F32)).astype(BF16)
    gc = _dot(tri, lf_hi) + _dot(tri, lf_lo)
    tot = gc[0:1] if rev else gc[cs - 1:cs]

    o = lax.dot_general((q * jnp.exp(gc)).astype(BF16), st.astype(BF16), _TRANS_B, preferred_element_type=F32)
    kd = (k * jnp.exp(tot - gc)).astype(BF16)
    st_new = jnp.exp(tot) * st + lax.dot_general(v, kd, _TRANS_A, preferred_element_type=F32)

    vf = v.astype(F32)
    r8 = row & (GLA_DIAG - 1)
    for j in range(GLA_DIAG):
        if j == 0:
            w, vj = q * k, vf
        else:
            sh = (cs - j) if rev else j
            w = q * pltpu.roll(k, sh, 0) * jnp.exp(jnp.minimum(gc - pltpu.roll(gc, sh, 0), 0.0))
            vj = pltpu.roll(vf, sh, 0)
        ok = (r8 <= GLA_DIAG - 1 - j) if rev else (r8 >= j)
        o += jnp.where(ok, jnp.sum(w, axis=-1, keepdims=True), 0.0) * vj

    attn = jnp.zeros((cs, cs), F32)
    h = GLA_DIAG
    while h < cs:
        lg = h.bit_length() - 1
        bounds = [p * 2 * h + (h if rev else h - 1) for p in range(cs // (2 * h))]
        ref = jnp.concatenate([jnp.broadcast_to(gc[b:b + 1], (2 * h, gc.shape[1])) for b in bounds], axis=0)
        e = gc - ref
        q_side = ((row >> lg) & 1) == (0 if rev else 1)
        x = jnp.exp(jnp.where(q_side, e, -e))
        qt = jnp.where(q_side, q * x, 0.0).astype(BF16)
        kt = jnp.where(q_side, 0.0, k * x).astype(BF16)
        pr = lax.dot_general(qt, kt, _TRANS_B, preferred_element_type=F32)
        attn += jnp.where((row >> (lg + 1)) == (col >> (lg + 1)), pr, 0.0)
        h *= 2
    o += _dot(attn.astype(BF16), v)
    return o, st_new


def _gla_kernel(qs_ref, lff_ref, lfb_ref, kf_ref, kb_ref, v_ref, gs_ref, og_ref, y_ref,
                of_scr, ob_scr, st_scr):
    s_len = qs_ref.shape[1]
    cs = min(GLA_CHUNK, s_len)
    nc = s_len // cs
    row = lax.broadcasted_iota(jnp.int32, (cs, 1), 0)
    col = lax.broadcasted_iota(jnp.int32, (1, cs), 1)
    tri_f = (col <= row).astype(BF16)
    tri_b = (col >= row).astype(BF16)
    st_scr[...] = jnp.zeros_like(st_scr)

    def body(c, carry):
        for rev, lf_ref, k_ref, o_scr, tri in ((False, lff_ref, kf_ref, of_scr, tri_f),
                                              (True, lfb_ref, kb_ref, ob_scr, tri_b)):
            chunk = (nc - 1 - c) if rev else c
            rows = pl.ds(pl.multiple_of(chunk * cs, cs), cs)
            o, st = _gla_chunk(qs_ref[0, rows, :].astype(F32), k_ref[0, rows, :].astype(F32), v_ref[0, rows, :],
                               lf_ref[0, rows, :], st_scr[int(rev)], tri, row, col, rev)
            o_scr[rows, :] = o
            st_scr[int(rev)] = st
        return carry

    lax.fori_loop(0, nc, body, 0)
    o = of_scr[...] + ob_scr[...]
    y = o * lax.rsqrt(jnp.mean(o * o, axis=-1, keepdims=True) + EPS) * og_ref[...]
    y_ref[0] = (y * gs_ref[0].astype(F32)).astype(BF16)


def _gla_call(qs, lff, lfb, kf, kb, v, gs, out_g):
    bsz, s, hk = qs.shape
    blk = pl.BlockSpec((1, s, LANES), lambda b, h: (b, 0, h))
    return pl.pallas_call(
        _gla_kernel,
        grid=(bsz, hk // LANES),
        in_specs=[blk] * 7 + [pl.BlockSpec((1, LANES), lambda b, h: (0, 0))],
        out_specs=blk,
        out_shape=jax.ShapeDtypeStruct((bsz, s, hk), BF16),
        scratch_shapes=[pltpu.VMEM((s, LANES), F32), pltpu.VMEM((s, LANES), F32),
                        pltpu.VMEM((2, LANES, LANES), F32)],
        compiler_params=_params("arbitrary", "arbitrary"),
        name="gla",
    )(qs, lff, lfb, kf, kb, v, gs, out_g.reshape(1, LANES))


def kernel(x, c, ada_w, ada_b, norm_g, ffn_up, ffn_down, even_w_in, even_w_out, qk_norm_g, diff_lambda,
           diff_subln_g, sink_logit, rel_bias, odd_w_in, odd_w_out, c_lower_bound, c_out_norm_g):
    bsz, s, d = x.shape
    depth = ada_w.shape[0]
    assert d == 2 * A_HEADS * LANES == B_Q_HEADS * LANES == C_HEADS * LANES
    assert s % 256 == 0 or s == LANES

    mod_all = _mod_call(c, ada_w, ada_b)
    tiles_a, tiles_b = _bias_tiles(rel_bias)
    for l in range(depth):
        mod = mod_all[l].reshape(bsz, N_MOD, 1, d)
        sh1, sc1, g1, sh2, sc2, g2, sh3, sc3, g3 = (mod[:, k] for k in range(N_MOD))
        x = _ffn_call(x, norm_g[l, 0], sh1, sc1, g1, ffn_up[l, 0].astype(BF16), ffn_down[l, 0].astype(BF16))
        if l % 2 == 0:
            e = l // 2
            lam_init = 0.8 - 0.6 * math.exp(-0.3 * l)
            qkv = _even_in_call(x, norm_g[l, 1], sh2, sc2, even_w_in[e].astype(BF16))
            ya = _diff_attn_call(qkv, tiles_a, qk_norm_g[e], diff_lambda[e].astype(F32), diff_subln_g[e], lam_init)
            yb = _win_attn_call(qkv, tiles_b, qk_norm_g[e], sink_logit[e])
            x = _out_call(x, g2, [ya, yb], even_w_out[e].astype(BF16))
        else:
            o = l // 2
            parts = _odd_in_call(x, norm_g[l, 1], sh2, sc2, odd_w_in[o].astype(BF16),
                                 c_lower_bound.astype(F32), l)
            y = _gla_call(*parts, c_out_norm_g[o])
            x = _out_call(x, g2, [y], odd_w_out[o].astype(BF16))
        x = _ffn_call(x, norm_g[l, 2], sh3, sc3, g3, ffn_up[l, 1].astype(BF16), ffn_down[l, 1].astype(BF16))
    return x
```

```python
import functools
import math

import jax
import jax.numpy as jnp
from jax import lax
from jax.experimental import pallas as pl
from jax.experimental.pallas import tpu as pltpu

F32 = jnp.float32
BF16 = jnp.bfloat16

EPS = 1e-6
N_MOD = 9
HEAD_DIM = 64
LANES = 128
A_HEADS = 4
B_Q_HEADS = 8
B_KV_HEADS = 2
WINDOW = 128
NUM_BUCKETS = 32
MAX_DISTANCE = 128
C_HEADS = 8
FFN_CHUNK = 1024
GLA_CHUNK = 128
GLA_DIAG = 8
V7X_VMEM_LIMIT_BYTES = 56 * 1024 * 1024

_TRANS_B = (((1,), (1,)), ((), ()))
_TRANS_A = (((0,), (0,)), ((), ()))


def _params(*sem):
    return pltpu.CompilerParams(dimension_semantics=sem, vmem_limit_bytes=V7X_VMEM_LIMIT_BYTES)


def _dot(a, b):
    return jnp.dot(a, b, preferred_element_type=F32)


def _silu(t):
    return t / (1.0 + jnp.exp(-t))


def _norm_mod(x, g, shift, scale):
    ms = jnp.mean(x * x, axis=-1, keepdims=True)
    y = x * lax.rsqrt(ms + EPS) * g
    return y * (1.0 + scale) + shift


def _norm_halves(x, g, first):
    sq = x * x
    s0 = jnp.sum(jnp.where(first, sq, 0.0), axis=-1, keepdims=True)
    s1 = jnp.sum(jnp.where(first, 0.0, sq), axis=-1, keepdims=True)
    ms = jnp.where(first, s0, s1) * (1.0 / HEAD_DIM)
    return x * lax.rsqrt(ms + EPS) * g


def _mod_kernel(c_ref, w_ref, b_ref, o_ref):
    sc = _silu(c_ref[...])
    o_ref[0] = jnp.dot(sc, w_ref[0], precision=lax.Precision.HIGHEST,
                       preferred_element_type=F32) + b_ref[0]


def _mod_call(c, ada_w, ada_b):
    depth, d, m = ada_w.shape
    bsz = c.shape[0]
    tn = m // 8
    return pl.pallas_call(
        _mod_kernel,
        grid=(depth, m // tn),
        in_specs=[pl.BlockSpec((bsz, d), lambda l, j: (0, 0)),
                  pl.BlockSpec((1, d, tn), lambda l, j: (l, 0, j)),
                  pl.BlockSpec((1, 1, tn), lambda l, j: (l, 0, j))],
        out_specs=pl.BlockSpec((1, bsz, tn), lambda l, j: (l, 0, j)),
        out_shape=jax.ShapeDtypeStruct((depth, bsz, m), F32),
        compiler_params=_params("arbitrary", "arbitrary"),
        name="adaln_mod",
    )(c, ada_w, ada_b.reshape(depth, 1, m))


def _ffn_kernel(x_ref, g_ref, sh_ref, sc_ref, gate_ref, wu_ref, wd_ref, o_ref, *, chunks):
    x = x_ref[0]
    h = _norm_mod(x, g_ref[...], sh_ref[0], sc_ref[0]).astype(BF16)
    f = wd_ref.shape[0]
    acc = None
    for c0, cw in chunks:
        a = _dot(h, wu_ref[:, c0:c0 + cw])
        b = _dot(h, wu_ref[:, f + c0:f + c0 + cw])
        part = _dot((_silu(a) * b).astype(BF16), wd_ref[c0:c0 + cw, :])
        acc = part if acc is None else acc + part
    o_ref[0] = x + 0.5 * gate_ref[0] * acc


def _ffn_call(x, g, shift, scale, gate, w_up, w_down):
    bsz, s, d = x.shape
    f = w_down.shape[0]
    tm = min(512, s)
    chunks = tuple((c0, min(FFN_CHUNK, f - c0)) for c0 in range(0, f, FFN_CHUNK))
    row = lambda b, i: (b, i, 0)
    vec = lambda b, i: (b, 0, 0)
    resident = dict(pipeline_mode=pl.Buffered(1))
    return pl.pallas_call(
        functools.partial(_ffn_kernel, chunks=chunks),
        grid=(bsz, s // tm),
        in_specs=[pl.BlockSpec((1, tm, d), row),
                  pl.BlockSpec((1, d), lambda b, i: (0, 0)),
                  pl.BlockSpec((1, 1, d), vec),
                  pl.BlockSpec((1, 1, d), vec),
                  pl.BlockSpec((1, 1, d), vec),
                  pl.BlockSpec(w_up.shape, lambda b, i: (0, 0), **resident),
                  pl.BlockSpec(w_down.shape, lambda b, i: (0, 0), **resident)],
        out_specs=pl.BlockSpec((1, tm, d), row),
        out_shape=jax.ShapeDtypeStruct(x.shape, F32),
        compiler_params=_params("arbitrary", "arbitrary"),
        name="ffn",
    )(x, g.reshape(1, d), shift, scale, gate, w_up, w_down)


def _even_in_kernel(x_ref, g_ref, sh_ref, sc_ref, w_ref, o_ref):
    h = _norm_mod(x_ref[0], g_ref[...], sh_ref[0], sc_ref[0]).astype(BF16)
    o_ref[0] = _dot(h, w_ref[...]).astype(BF16)


def _even_in_call(x, g, shift, scale, w):
    bsz, s, d = x.shape
    n = w.shape[1]
    tm = min(512, s)
    row = lambda b, i: (b, i, 0)
    vec = lambda b, i: (b, 0, 0)
    return pl.pallas_call(
        _even_in_kernel,
        grid=(bsz, s // tm),
        in_specs=[pl.BlockSpec((1, tm, d), row),
                  pl.BlockSpec((1, d), lambda b, i: (0, 0)),
                  pl.BlockSpec((1, 1, d), vec),
                  pl.BlockSpec((1, 1, d), vec),
                  pl.BlockSpec((d, n), lambda b, i: (0, 0))],
        out_specs=pl.BlockSpec((1, tm, n), row),
        out_shape=jax.ShapeDtypeStruct((bsz, s, n), BF16),
        compiler_params=_params("arbitrary", "arbitrary"),
        name="even_in_proj",
    )(x, g.reshape(1, d), shift, scale, w)


def _log_forget_and_input_gate(f_raw, clb, layer):
    e = jnp.exp(clb - jnp.max(clb, axis=0, keepdims=True))
    lb = jnp.sum(e[1:layer + 1], axis=0, keepdims=True) / jnp.sum(e, axis=0, keepdims=True)
    t = jnp.exp(-jnp.abs(f_raw))
    r = 1.0 / (1.0 + t)
    pos = f_raw >= 0.0
    f = lb + (1.0 - lb) * (jnp.where(pos, 1.0, t) * r)
    lf = jnp.where(f > 0.0, jnp.log(f), jnp.log(1.0 - lb) + f_raw)
    return lf, (1.0 - lb) * (jnp.where(pos, t, 1.0) * r)


def _odd_in_kernel(x_ref, g_ref, sh_ref, sc_ref, w_ref, clb_ref,
                   qs_ref, lff_ref, lfb_ref, kf_ref, kb_ref, v_ref, gs_ref, h_scr, *, layer):
    j = pl.program_id(2)

    @pl.when(j == 0)
    def _():
        h_scr[...] = _norm_mod(x_ref[0], g_ref[...], sh_ref[0], sc_ref[0]).astype(BF16)

    y = _dot(h_scr[...], w_ref[...])

    @pl.when(j == 0)
    def _():
        qs_ref[0] = _silu(y).astype(BF16)

    @pl.when(j == 1)
    def _():
        lf, k = _log_forget_and_input_gate(y, clb_ref[0], layer)
        lff_ref[0] = lf
        kf_ref[0] = k.astype(BF16)

    @pl.when(j == 2)
    def _():
        lf, k = _log_forget_and_input_gate(y, clb_ref[1], layer)
        lfb_ref[0] = lf
        kb_ref[0] = k.astype(BF16)

    @pl.when(j == 3)
    def _():
        v_ref[0] = y.astype(BF16)

    @pl.when(j == 4)
    def _():
        gs_ref[0] = _silu(y).astype(BF16)


def _odd_in_call(x, g, shift, scale, w, clb, layer):
    bsz, s, d = x.shape
    hk = clb.shape[-1]
    tm = min(512, s)
    row = lambda b, i, j: (b, i, 0)
    vec = lambda b, i, j: (b, 0, 0)
    out = lambda dt: jax.ShapeDtypeStruct((bsz, s, hk), dt)
    return pl.pallas_call(
        functools.partial(_odd_in_kernel, layer=layer),
        grid=(bsz, s // tm, 5),
        in_specs=[pl.BlockSpec((1, tm, d), row),
                  pl.BlockSpec((1, d), lambda b, i, j: (0, 0)),
                  pl.BlockSpec((1, 1, d), vec),
                  pl.BlockSpec((1, 1, d), vec),
                  pl.BlockSpec((d, hk), lambda b, i, j: (0, j)),
                  pl.BlockSpec(clb.shape, lambda b, i, j: (0, 0, 0))],
        out_specs=[pl.BlockSpec((1, tm, hk), row)] * 7,
        out_shape=[out(BF16), out(F32), out(F32), out(BF16), out(BF16), out(BF16), out(BF16)],
        scratch_shapes=[pltpu.VMEM((tm, d), BF16)],
        compiler_params=_params("arbitrary", "arbitrary", "arbitrary"),
        name="odd_in_proj",
    )(x, g.reshape(1, d), shift, scale, w, clb)


def _out_kernel(*refs, n_in):
    x_ref, gate_ref = refs[0], refs[1]
    y_refs = refs[2:2 + n_in]
    w_refs = refs[2 + n_in:2 + 2 * n_in]
    o_ref = refs[2 + 2 * n_in]
    acc = _dot(y_refs[0][0], w_refs[0][...])
    for y_ref, w_ref in zip(y_refs[1:], w_refs[1:]):
        acc += _dot(y_ref[0], w_ref[...])
    o_ref[0] = x_ref[0] + gate_ref[0] * acc


def _out_call(x, gate, ys, w):
    bsz, s, d = x.shape
    tm = min(512, s)
    n_in = len(ys)
    wk = ys[0].shape[-1]
    row = lambda b, i: (b, i, 0)
    return pl.pallas_call(
        functools.partial(_out_kernel, n_in=n_in),
        grid=(bsz, s // tm),
        in_specs=([pl.BlockSpec((1, tm, d), row), pl.BlockSpec((1, 1, d), lambda b, i: (b, 0, 0))]
                  + [pl.BlockSpec((1, tm, wk), row)] * n_in
                  + [pl.BlockSpec((wk, d), functools.partial(lambda b, i, k: (k, 0), k=k)) for k in range(n_in)]),
        out_specs=pl.BlockSpec((1, tm, d), row),
        out_shape=jax.ShapeDtypeStruct(x.shape, F32),
        compiler_params=_params("arbitrary", "arbitrary"),
        name="out_proj",
    )(x, gate, *ys, *([w] * n_in))


def _t5_bucket(rel):
    half = NUM_BUCKETS // 2
    max_exact = half // 2
    n = jnp.abs(rel)
    nf = jnp.maximum(n, 1).astype(F32)
    large = max_exact + (jnp.log(nf / max_exact) / math.log(MAX_DISTANCE / max_exact)
                         * (half - max_exact)).astype(jnp.int32)
    large = jnp.minimum(large, half - 1)
    return jnp.where(rel > 0, half, 0) + jnp.where(n < max_exact, n, large)


def _bias_tiles(rel_bias):
    r = jnp.arange(LANES)[:, None]
    c = jnp.arange(LANES)[None, :]

    def lookup(rel, table):
        hot = jax.nn.one_hot(_t5_bucket(rel), NUM_BUCKETS, dtype=F32)
        return jnp.einsum('...k,kh->h...', hot, table.astype(F32), precision=lax.Precision.HIGHEST)

    rel_a = (jnp.arange(-2, 3) * LANES)[:, None, None] + (c - r)[None]
    rel_b = jnp.arange(3 * LANES)[None, :] - LANES - r
    return lookup(rel_a, rel_bias[:, :A_HEADS]), lookup(rel_b, rel_bias[:, A_HEADS:])


def _diff_attn_kernel(q_ref, k_ref, v_ref, bias_ref, qg_ref, kg_ref, lam_ref, sg_ref, o_ref,
                      kn_scr, bias_scr, *, lam_init):
    i = pl.program_id(2)
    tq = q_ref.shape[1]
    s_len = k_ref.shape[1]
    first = lax.broadcasted_iota(jnp.int32, (1, LANES), 1) < HEAD_DIM

    @pl.when(i == 0)
    def _():
        kn_scr[...] = _norm_halves(k_ref[0].astype(F32), kg_ref[...], first).astype(BF16)

    for a in range(tq // LANES):
        qb = i * (tq // LANES) + a
        for j in range(s_len // LANES):
            idx = jnp.clip(j - qb, -2, 2) + 2
            bias_scr[a * LANES:(a + 1) * LANES, j * LANES:(j + 1) * LANES] = bias_ref[0, idx]

    lp = lam_ref[...]
    lam = (jnp.exp(jnp.sum(lp[0:1] * lp[1:2], axis=-1, keepdims=True))
           - jnp.exp(jnp.sum(lp[2:3] * lp[3:4], axis=-1, keepdims=True)) + lam_init)

    qn = _norm_halves(q_ref[0].astype(F32), qg_ref[...], first) * (HEAD_DIM ** -0.5)
    kn = kn_scr[...]
    bias = bias_scr[...]

    def softmax_rows(qm):
        sc = lax.dot_general(qm.astype(BF16), kn, _TRANS_B, preferred_element_type=F32) + bias
        e = jnp.exp(sc - jnp.max(sc, axis=-1, keepdims=True))
        return e * (1.0 / jnp.sum(e, axis=-1, keepdims=True))

    p = softmax_rows(jnp.where(first, qn, 0.0)) - lam * softmax_rows(jnp.where(first, 0.0, qn))
    o = _dot(p.astype(BF16), v_ref[0])
    o = o * lax.rsqrt(jnp.mean(o * o, axis=-1, keepdims=True) + EPS) * sg_ref[...] * (1.0 - lam_init)
    o_ref[0] = o.astype(BF16)


def _diff_attn_call(qkv, tiles_a, qk_g, lam_p, subln_g, lam_init):
    bsz, s, _ = qkv.shape
    tq = min(256, s)
    dup = lambda g: jnp.concatenate([g, g]).reshape(1, LANES)
    return pl.pallas_call(
        functools.partial(_diff_attn_kernel, lam_init=lam_init),
        grid=(bsz, A_HEADS, s // tq),
        in_specs=[pl.BlockSpec((1, tq, LANES), lambda b, h, i: (b, i, h)),
                  pl.BlockSpec((1, s, LANES), lambda b, h, i: (b, 0, A_HEADS + h)),
                  pl.BlockSpec((1, s, LANES), lambda b, h, i: (b, 0, 2 * A_HEADS + h)),
                  pl.BlockSpec((1, 5, LANES, LANES), lambda b, h, i: (h, 0, 0, 0)),
                  pl.BlockSpec((1, LANES), lambda b, h, i: (0, 0)),
                  pl.BlockSpec((1, LANES), lambda b, h, i: (0, 0)),
                  pl.BlockSpec(lam_p.shape, lambda b, h, i: (0, 0)),
                  pl.BlockSpec((1, LANES), lambda b, h, i: (0, 0))],
        out_specs=pl.BlockSpec((1, tq, LANES), lambda b, h, i: (b, i, h)),
        out_shape=jax.ShapeDtypeStruct((bsz, s, A_HEADS * LANES), BF16),
        scratch_shapes=[pltpu.VMEM((s, LANES), BF16), pltpu.VMEM((tq, s), F32)],
        compiler_params=_params("arbitrary", "arbitrary", "arbitrary"),
        name="diff_attn",
    )(qkv, qkv, qkv, tiles_a, dup(qk_g[0]), dup(qk_g[1]), lam_p, subln_g.reshape(1, LANES))


def _win_attn_kernel(q_ref, k_ref, v_ref, bias_ref, qg_ref, kg_ref, sink_ref, o_ref, kn_scr, *, q_blocks):
    i = pl.program_id(1)
    nb = pl.num_programs(1) * q_blocks
    group = B_Q_HEADS // B_KV_HEADS
    first = lax.broadcasted_iota(jnp.int32, (1, LANES), 1) < HEAD_DIM

    @pl.when(i == 0)
    def _():
        kn_scr[...] = _norm_halves(k_ref[0].astype(F32), kg_ref[...], first).astype(BF16)

    r = lax.broadcasted_iota(jnp.int32, (group * LANES, 1), 0) & (LANES - 1)
    c = lax.broadcasted_iota(jnp.int32, (1, 3 * LANES), 1)
    band = jnp.abs(c - LANES - r) <= WINDOW
    sinks = [jnp.max(sink_ref[kv], axis=-1, keepdims=True) for kv in range(B_KV_HEADS)]

    for u in range(q_blocks):
        blk = i * q_blocks + u
        rows = slice(u * LANES, (u + 1) * LANES)
        starts = [pl.multiple_of(j * LANES, LANES)
                  for j in (jnp.maximum(blk - 1, 0), blk, jnp.minimum(blk + 1, nb - 1))]
        k3 = jnp.concatenate([kn_scr[pl.ds(st, LANES), :] for st in starts], axis=0)
        v3 = jnp.concatenate([v_ref[0, pl.ds(st, LANES), :] for st in starts], axis=0)
        valid = band & ((c >= LANES) | (blk > 0)) & ((c < 2 * LANES) | (blk < nb - 1))

        pairs = []
        for pb in range(B_Q_HEADS // 2):
            qp = _norm_halves(q_ref[0, rows, pb * LANES:(pb + 1) * LANES].astype(F32), qg_ref[...], first)
            qp = qp * (HEAD_DIM ** -0.5)
            pairs.append((qp, pltpu.roll(qp, HEAD_DIM, 1)))

        for kv in range(B_KV_HEADS):
            kv_lanes = first if kv == 0 else jnp.logical_not(first)
            pbs = range(kv * group // 2, (kv + 1) * group // 2)
            qs = jnp.concatenate([jnp.where(kv_lanes, pairs[pb][0 if a == kv else 1], 0.0).astype(BF16)
                                  for pb in pbs for a in range(2)], axis=0)
            sc = lax.dot_general(qs, k3, _TRANS_B, preferred_element_type=F32) + bias_ref[kv]
            sc = jnp.where(valid, sc, -jnp.inf)
            sk = sinks[kv]
            m = jnp.maximum(jnp.max(sc, axis=-1, keepdims=True), sk)
            e = jnp.exp(sc - m)
            p = e * (1.0 / (jnp.sum(e, axis=-1, keepdims=True) + jnp.exp(sk - m)))
            o = _dot(p.astype(BF16), v3)
            for n, pb in enumerate(pbs):
                halves = []
                for a in range(2):
                    oh = o[(2 * n + a) * LANES:(2 * n + a + 1) * LANES]
                    halves.append(oh if a == kv else pltpu.roll(oh, HEAD_DIM, 1))
                o_ref[0, rows, pb * LANES:(pb + 1) * LANES] = jnp.where(first, halves[0], halves[1]).astype(BF16)


def _win_attn_call(qkv, tiles_b, qk_g, sink):
    bsz, s, n = qkv.shape
    qw = B_Q_HEADS * HEAD_DIM
    group = B_Q_HEADS // B_KV_HEADS
    q_blocks = min(4, s // LANES)
    tq = q_blocks * LANES
    q_blk = (2 * A_HEADS * LANES + A_HEADS * LANES) // qw
    k_blk = (2 * A_HEADS * LANES + A_HEADS * LANES + qw) // LANES
    dup = lambda g: jnp.concatenate([g, g]).reshape(1, LANES)
    bias = tiles_b.reshape(B_KV_HEADS, group * LANES, 3 * LANES)
    sink_rows = jnp.broadcast_to(jnp.repeat(sink.astype(F32).reshape(B_KV_HEADS, group), LANES, axis=1)[..., None],
                                 (B_KV_HEADS, group * LANES, LANES))
    return pl.pallas_call(
        functools.partial(_win_attn_kernel, q_blocks=q_blocks),
        grid=(bsz, s // tq),
        in_specs=[pl.BlockSpec((1, tq, qw), lambda b, i: (b, i, q_blk)),
                  pl.BlockSpec((1, s, LANES), lambda b, i: (b, 0, k_blk)),
                  pl.BlockSpec((1, s, LANES), lambda b, i: (b, 0, k_blk + 1)),
                  pl.BlockSpec(bias.shape, lambda b, i: (0, 0, 0)),
                  pl.BlockSpec((1, LANES), lambda b, i: (0, 0)),
                  pl.BlockSpec((1, LANES), lambda b, i: (0, 0)),
                  pl.BlockSpec(sink_rows.shape, lambda b, i: (0, 0, 0))],
        out_specs=pl.BlockSpec((1, tq, qw), lambda b, i: (b, i, 0)),
        out_shape=jax.ShapeDtypeStruct((bsz, s, qw), BF16),
        scratch_shapes=[pltpu.VMEM((s, LANES), BF16)],
        compiler_params=_params("arbitrary", "arbitrary"),
        name="win_attn",
    )(qkv, qkv, qkv, bias, dup(qk_g[2]), dup(qk_g[3]), sink_rows)


def _gla_chunk(q, k, v, lf, st, tri, row, col, rev):
    cs = q.shape[0]
    lf_hi = lf.astype(BF16)
    lf_lo = (lf - lf_hi.astype(F32)).astype(BF16)
    gc = _dot(tri, lf_hi) + _dot(tri, lf_lo)
    tot = gc[0:1] if rev else gc[cs - 1:cs]

    o = lax.dot_general((q * jnp.exp(gc)).astype(BF16), st.astype(BF16), _TRANS_B, preferred_element_type=F32)
    kd = (k * jnp.exp(tot - gc)).astype(BF16)
    st_new = jnp.exp(tot) * st + lax.dot_general(v, kd, _TRANS_A, preferred_element_type=F32)

    vf = v.astype(F32)
    r8 = row & (GLA_DIAG - 1)
    for j in range(GLA_DIAG):
        if j == 0:
            w, vj = q * k, vf
        else:
            sh = (cs - j) if rev else j
            w = q * pltpu.roll(k, sh, 0) * jnp.exp(jnp.minimum(gc - pltpu.roll(gc, sh, 0), 0.0))
            vj = pltpu.roll(vf, sh, 0)
        ok = (r8 <= GLA_DIAG - 1 - j) if rev else (r8 >= j)
        o += jnp.where(ok, jnp.sum(w, axis=-1, keepdims=True), 0.0) * vj

    attn = jnp.zeros((cs, cs), F32)
    h = GLA_DIAG
    while h < cs:
        lg = h.bit_length() - 1
        bounds = [p * 2 * h + (h if rev else h - 1) for p in range(cs // (2 * h))]
        ref = jnp.concatenate([jnp.broadcast_to(gc[b:b + 1], (2 * h, gc.shape[1])) for b in bounds], axis=0)
        e = gc - ref
        q_side = ((row >> lg) & 1) == (0 if rev else 1)
        x = jnp.exp(jnp.where(q_side, e, -e))
        qt = jnp.where(q_side, q * x, 0.0).astype(BF16)
        kt = jnp.where(q_side, 0.0, k * x).astype(BF16)
        pr = lax.dot_general(qt, kt, _TRANS_B, preferred_element_type=F32)
        attn += jnp.where((row >> (lg + 1)) == (col >> (lg + 1)), pr, 0.0)
        h *= 2
    o += _dot(attn.astype(BF16), v)
    return o, st_new


def _gla_kernel(qs_ref, lff_ref, lfb_ref, kf_ref, kb_ref, v_ref, gs_ref, og_ref, y_ref,
                of_scr, ob_scr, st_scr):
    s_len = qs_ref.shape[1]
    cs = min(GLA_CHUNK, s_len)
    nc = s_len // cs
    row = lax.broadcasted_iota(jnp.int32, (cs, 1), 0)
    col = lax.broadcasted_iota(jnp.int32, (1, cs), 1)
    tri_f = (col <= row).astype(BF16)
    tri_b = (col >= row).astype(BF16)
    st_scr[...] = jnp.zeros_like(st_scr)

    def body(c, carry):
        for rev, lf_ref, k_ref, o_scr, tri in ((False, lff_ref, kf_ref, of_scr, tri_f),
                                              (True, lfb_ref, kb_ref, ob_scr, tri_b)):
            chunk = (nc - 1 - c) if rev else c
            rows = pl.ds(pl.multiple_of(chunk * cs, cs), cs)
            o, st = _gla_chunk(qs_ref[0, rows, :].astype(F32), k_ref[0, rows, :].astype(F32), v_ref[0, rows, :],
                               lf_ref[0, rows, :], st_scr[int(rev)], tri, row, col, rev)
            o_scr[rows, :] = o
            st_scr[int(rev)] = st
        return carry

    lax.fori_loop(0, nc, body, 0)
    o = of_scr[...] + ob_scr[...]
    y = o * lax.rsqrt(jnp.mean(o * o, axis=-1, keepdims=True) + EPS) * og_ref[...]
    y_ref[0] = (y * gs_ref[0].astype(F32)).astype(BF16)


def _gla_call(qs, lff, lfb, kf, kb, v, gs, out_g):
    bsz, s, hk = qs.shape
    blk = pl.BlockSpec((1, s, LANES), lambda b, h: (b, 0, h))
    return pl.pallas_call(
        _gla_kernel,
        grid=(bsz, hk // LANES),
        in_specs=[blk] * 7 + [pl.BlockSpec((1, LANES), lambda b, h: (0, 0))],
        out_specs=blk,
        out_shape=jax.ShapeDtypeStruct((bsz, s, hk), BF16),
        scratch_shapes=[pltpu.VMEM((s, LANES), F32), pltpu.VMEM((s, LANES), F32),
                        pltpu.VMEM((2, LANES, LANES), F32)],
        compiler_params=_params("arbitrary", "arbitrary"),
        name="gla",
    )(qs, lff, lfb, kf, kb, v, gs, out_g.reshape(1, LANES))


def kernel(x, c, ada_w, ada_b, norm_g, ffn_up, ffn_down, even_w_in, even_w_out, qk_norm_g, diff_lambda,
           diff_subln_g, sink_logit, rel_bias, odd_w_in, odd_w_out, c_lower_bound, c_out_norm_g):
    bsz, s, d = x.shape
    depth = ada_w.shape[0]
    assert d == 2 * A_HEADS * LANES == B_Q_HEADS * LANES == C_HEADS * LANES
    assert s % 256 == 0 or s == LANES

    mod_all = _mod_call(c, ada_w, ada_b)
    tiles_a, tiles_b = _bias_tiles(rel_bias)
    for l in range(depth):
        mod = mod_all[l].reshape(bsz, N_MOD, 1, d)
        sh1, sc1, g1, sh2, sc2, g2, sh3, sc3, g3 = (mod[:, k] for k in range(N_MOD))
        x = _ffn_call(x, norm_g[l, 0], sh1, sc1, g1, ffn_up[l, 0].astype(BF16), ffn_down[l, 0].astype(BF16))
        if l % 2 == 0:
            e = l // 2
            lam_init = 0.8 - 0.6 * math.exp(-0.3 * l)
            qkv = _even_in_call(x, norm_g[l, 1], sh2, sc2, even_w_in[e].astype(BF16))
            ya = _diff_attn_call(qkv, tiles_a, qk_norm_g[e], diff_lambda[e].astype(F32), diff_subln_g[e], lam_init)
            yb = _win_attn_call(qkv, tiles_b, qk_norm_g[e], sink_logit[e])
            x = _out_call(x, g2, [ya, yb], even_w_out[e].astype(BF16))
        else:
            o = l // 2
            parts = _odd_in_call(x, norm_g[l, 1], sh2, sc2, odd_w_in[o].astype(BF16),
                                 c_lower_bound.astype(F32), l)
            y = _gla_call(*parts, c_out_norm_g[o])
            x = _out_call(x, g2, [y], odd_w_out[o].astype(BF16))
        x = _ffn_call(x, norm_g[l, 2], sh3, sc3, g3, ffn_up[l, 1].astype(BF16), ffn_down[l, 1].astype(BF16))
    return x
```

```python
import functools
import math

import jax
import jax.numpy as jnp
import numpy as np
from jax import lax
from jax.experimental import pallas as pl
from jax.experimental.pallas import tpu as pltpu

F32 = jnp.float32
BF16 = jnp.bfloat16

EPS = 1e-6
N_MOD = 9
HEAD_DIM = 64
LANES = 128
A_HEADS = 4
B_Q_HEADS = 8
B_KV_HEADS = 2
WINDOW = 128
NUM_BUCKETS = 32
MAX_DISTANCE = 128
C_HEADS = 8
FFN_CHUNK = 1024
SUBLANES = 8
GLA_CHUNK = 128
GLA_HEADS_PER_STEP = 2
LOG2E = math.log2(math.e)
V7X_VMEM_LIMIT_BYTES = 56 * 1024 * 1024

_TRANS_B = (((1,), (1,)), ((), ()))
_TRANS_A = (((0,), (0,)), ((), ()))


def _params(*sem):
    return pltpu.CompilerParams(dimension_semantics=sem, vmem_limit_bytes=V7X_VMEM_LIMIT_BYTES)


def _dot(a, b):
    return jnp.dot(a, b, preferred_element_type=F32)


def _silu(t):
    return t / (1.0 + jnp.exp(-t))


def _norm_mod(x, g, shift, scale):
    ms = jnp.mean(x * x, axis=-1, keepdims=True)
    y = x * lax.rsqrt(ms + EPS) * g
    return y * (1.0 + scale) + shift


def _norm_halves(x, g, first):
    sq = x * x
    s0 = jnp.sum(jnp.where(first, sq, 0.0), axis=-1, keepdims=True)
    s1 = jnp.sum(jnp.where(first, 0.0, sq), axis=-1, keepdims=True)
    ms = jnp.where(first, s0, s1) * (1.0 / HEAD_DIM)
    return x * lax.rsqrt(ms + EPS) * g


def _mod_kernel(c_ref, w_ref, b_ref, o_ref):
    sc = _silu(c_ref[...])
    o_ref[0] = jnp.dot(sc, w_ref[0], precision=lax.Precision.HIGHEST,
                       preferred_element_type=F32) + b_ref[0]


def _mod_call(c, ada_w, ada_b):
    depth, d, m = ada_w.shape
    bsz = c.shape[0]
    tn = m // 8
    return pl.pallas_call(
        _mod_kernel,
        grid=(depth, m // tn),
        in_specs=[pl.BlockSpec((bsz, d), lambda l, j: (0, 0)),
                  pl.BlockSpec((1, d, tn), lambda l, j: (l, 0, j)),
                  pl.BlockSpec((1, 1, tn), lambda l, j: (l, 0, j))],
        out_specs=pl.BlockSpec((1, bsz, tn), lambda l, j: (l, 0, j)),
        out_shape=jax.ShapeDtypeStruct((depth, bsz, m), F32),
        compiler_params=_params("arbitrary", "arbitrary"),
        name="adaln_mod",
    )(c, ada_w, ada_b.reshape(depth, 1, m))


def _ffn_kernel(x_ref, g_ref, sh_ref, sc_ref, gate_ref, wu_ref, wd_ref, o_ref, *, chunks):
    x = x_ref[0]
    h = _norm_mod(x, g_ref[...], sh_ref[0], sc_ref[0]).astype(BF16)
    f = wd_ref.shape[0]
    acc = None
    for c0, cw in chunks:
        a = _dot(h, wu_ref[:, c0:c0 + cw])
        b = _dot(h, wu_ref[:, f + c0:f + c0 + cw])
        part = _dot((_silu(a) * b).astype(BF16), wd_ref[c0:c0 + cw, :])
        acc = part if acc is None else acc + part
    o_ref[0] = x + 0.5 * gate_ref[0] * acc


def _ffn_call(x, g, shift, scale, gate, w_up, w_down):
    bsz, s, d = x.shape
    f = w_down.shape[0]
    tm = min(512, s)
    chunks = tuple((c0, min(FFN_CHUNK, f - c0)) for c0 in range(0, f, FFN_CHUNK))
    row = lambda b, i: (b, i, 0)
    vec = lambda b, i: (b, 0, 0)
    resident = dict(pipeline_mode=pl.Buffered(1))
    return pl.pallas_call(
        functools.partial(_ffn_kernel, chunks=chunks),
        grid=(bsz, s // tm),
        in_specs=[pl.BlockSpec((1, tm, d), row),
                  pl.BlockSpec((1, d), lambda b, i: (0, 0)),
                  pl.BlockSpec((1, 1, d), vec),
                  pl.BlockSpec((1, 1, d), vec),
                  pl.BlockSpec((1, 1, d), vec),
                  pl.BlockSpec(w_up.shape, lambda b, i: (0, 0), **resident),
                  pl.BlockSpec(w_down.shape, lambda b, i: (0, 0), **resident)],
        out_specs=pl.BlockSpec((1, tm, d), row),
        out_shape=jax.ShapeDtypeStruct(x.shape, F32),
        compiler_params=_params("arbitrary", "arbitrary"),
        name="ffn",
    )(x, g.reshape(1, d), shift, scale, gate, w_up, w_down)


def _even_in_kernel(x_ref, g_ref, sh_ref, sc_ref, w_ref, o_ref):
    h = _norm_mod(x_ref[0], g_ref[...], sh_ref[0], sc_ref[0]).astype(BF16)
    o_ref[0] = _dot(h, w_ref[...]).astype(BF16)


def _even_in_call(x, g, shift, scale, w):
    bsz, s, d = x.shape
    n = w.shape[1]
    tm = min(512, s)
    row = lambda b, i: (b, i, 0)
    vec = lambda b, i: (b, 0, 0)
    return pl.pallas_call(
        _even_in_kernel,
        grid=(bsz, s // tm),
        in_specs=[pl.BlockSpec((1, tm, d), row),
                  pl.BlockSpec((1, d), lambda b, i: (0, 0)),
                  pl.BlockSpec((1, 1, d), vec),
                  pl.BlockSpec((1, 1, d), vec),
                  pl.BlockSpec((d, n), lambda b, i: (0, 0))],
        out_specs=pl.BlockSpec((1, tm, n), row),
        out_shape=jax.ShapeDtypeStruct((bsz, s, n), BF16),
        compiler_params=_params("arbitrary", "arbitrary"),
        name="even_in_proj",
    )(x, g.reshape(1, d), shift, scale, w)


def _log_forget_and_input_gate(f_raw, clb, layer):
    e = jnp.exp(clb - jnp.max(clb, axis=0, keepdims=True))
    lb = jnp.sum(e[1:layer + 1], axis=0, keepdims=True) / jnp.sum(e, axis=0, keepdims=True)
    t = jnp.exp(-jnp.abs(f_raw))
    r = 1.0 / (1.0 + t)
    pos = f_raw >= 0.0
    f = lb + (1.0 - lb) * (jnp.where(pos, 1.0, t) * r)
    lf = jnp.where(f > 0.0, jnp.log(f), jnp.log(1.0 - lb) + f_raw)
    return lf * LOG2E, (1.0 - lb) * (jnp.where(pos, t, 1.0) * r)


def _odd_in_kernel(x_ref, g_ref, sh_ref, sc_ref, w_ref, clb_ref,
                   qs_ref, lff_ref, lfb_ref, kf_ref, kb_ref, v_ref, gs_ref, *, layer):
    h = _norm_mod(x_ref[0], g_ref[...], sh_ref[0], sc_ref[0]).astype(BF16)
    hk = qs_ref.shape[-1]
    seg = lambda n: _dot(h, w_ref[:, n * hk:(n + 1) * hk])
    qs_ref[0] = _silu(seg(0)).astype(BF16)
    for n, lf_ref, k_ref in ((1, lff_ref, kf_ref), (2, lfb_ref, kb_ref)):
        lf2, k = _log_forget_and_input_gate(seg(n), clb_ref[n - 1], layer)
        lf_ref[0] = lf2
        k_ref[0] = k.astype(BF16)
    v_ref[0] = seg(3).astype(BF16)
    gs_ref[0] = _silu(seg(4)).astype(BF16)


def _odd_in_call(x, g, shift, scale, w, clb, layer):
    bsz, s, d = x.shape
    hk = clb.shape[-1]
    tm = min(512, s)
    row = lambda b, i: (b, i, 0)
    vec = lambda b, i: (b, 0, 0)
    out = lambda dt: jax.ShapeDtypeStruct((bsz, s, hk), dt)
    return pl.pallas_call(
        functools.partial(_odd_in_kernel, layer=layer),
        grid=(bsz, s // tm),
        in_specs=[pl.BlockSpec((1, tm, d), row),
                  pl.BlockSpec((1, d), lambda b, i: (0, 0)),
                  pl.BlockSpec((1, 1, d), vec),
                  pl.BlockSpec((1, 1, d), vec),
                  pl.BlockSpec(w.shape, lambda b, i: (0, 0), pipeline_mode=pl.Buffered(1)),
                  pl.BlockSpec(clb.shape, lambda b, i: (0, 0, 0))],
        out_specs=[pl.BlockSpec((1, tm, hk), row)] * 7,
        out_shape=[out(BF16), out(F32), out(F32), out(BF16), out(BF16), out(BF16), out(BF16)],
        compiler_params=_params("arbitrary", "arbitrary"),
        name="odd_in_proj",
    )(x, g.reshape(1, d), shift, scale, w, clb)


def _out_kernel(*refs, n_in):
    x_ref, gate_ref = refs[0], refs[1]
    y_refs = refs[2:2 + n_in]
    w_refs = refs[2 + n_in:2 + 2 * n_in]
    o_ref = refs[2 + 2 * n_in]
    acc = _dot(y_refs[0][0], w_refs[0][...])
    for y_ref, w_ref in zip(y_refs[1:], w_refs[1:]):
        acc += _dot(y_ref[0], w_ref[...])
    o_ref[0] = x_ref[0] + gate_ref[0] * acc


def _out_call(x, gate, ys, w):
    bsz, s, d = x.shape
    tm = min(512, s)
    n_in = len(ys)
    wk = ys[0].shape[-1]
    row = lambda b, i: (b, i, 0)
    return pl.pallas_call(
        functools.partial(_out_kernel, n_in=n_in),
        grid=(bsz, s // tm),
        in_specs=([pl.BlockSpec((1, tm, d), row), pl.BlockSpec((1, 1, d), lambda b, i: (b, 0, 0))]
                  + [pl.BlockSpec((1, tm, wk), row)] * n_in
                  + [pl.BlockSpec((wk, d), functools.partial(lambda b, i, k: (k, 0), k=k)) for k in range(n_in)]),
        out_specs=pl.BlockSpec((1, tm, d), row),
        out_shape=jax.ShapeDtypeStruct(x.shape, F32),
        compiler_params=_params("arbitrary", "arbitrary"),
        name="out_proj",
    )(x, gate, *ys, *([w] * n_in))


def _t5_bucket(rel):
    half = NUM_BUCKETS // 2
    max_exact = half // 2
    n = jnp.abs(rel)
    nf = jnp.maximum(n, 1).astype(F32)
    large = max_exact + (jnp.log(nf / max_exact) / math.log(MAX_DISTANCE / max_exact)
                         * (half - max_exact)).astype(jnp.int32)
    large = jnp.minimum(large, half - 1)
    return jnp.where(rel > 0, half, 0) + jnp.where(n < max_exact, n, large)


def _bias_tiles(rel_bias):
    r = jnp.arange(LANES)[:, None]
    c = jnp.arange(LANES)[None, :]

    def lookup(rel, table):
        hot = jax.nn.one_hot(_t5_bucket(rel), NUM_BUCKETS, dtype=F32)
        return jnp.einsum('...k,kh->h...', hot, table.astype(F32), precision=lax.Precision.HIGHEST)

    rel_a = (jnp.arange(-2, 3) * LANES)[:, None, None] + (c - r)[None]
    rel_b = jnp.arange(3 * LANES)[None, :] - LANES - r
    return lookup(rel_a, rel_bias[:, :A_HEADS]), lookup(rel_b, rel_bias[:, A_HEADS:])


def _diff_attn_kernel(q_ref, k_ref, v_ref, bias_ref, qg_ref, kg_ref, lam_ref, sg_ref, o_ref,
                      kn_scr, bias_scr, *, lam_init):
    i = pl.program_id(2)
    tq = q_ref.shape[1]
    s_len = k_ref.shape[1]
    first = lax.broadcasted_iota(jnp.int32, (1, LANES), 1) < HEAD_DIM

    @pl.when(i == 0)
    def _():
        kn_scr[...] = _norm_halves(k_ref[0].astype(F32), kg_ref[...], first).astype(BF16)

    for a in range(tq // LANES):
        qb = i * (tq // LANES) + a
        for j in range(s_len // LANES):
            idx = jnp.clip(j - qb, -2, 2) + 2
            bias_scr[a * LANES:(a + 1) * LANES, j * LANES:(j + 1) * LANES] = bias_ref[0, idx]

    lp = lam_ref[...]
    lam = (jnp.exp(jnp.sum(lp[0:1] * lp[1:2], axis=-1, keepdims=True))
           - jnp.exp(jnp.sum(lp[2:3] * lp[3:4], axis=-1, keepdims=True)) + lam_init)

    qn = _norm_halves(q_ref[0].astype(F32), qg_ref[...], first) * (HEAD_DIM ** -0.5 * LOG2E)
    kn = kn_scr[...]
    bias = bias_scr[...]

    def exp_rows(qm):
        sc = lax.dot_general(qm.astype(BF16), kn, _TRANS_B, preferred_element_type=F32) + bias
        e = jnp.exp2(sc - jnp.max(sc, axis=-1, keepdims=True))
        return e, 1.0 / jnp.sum(e, axis=-1, keepdims=True)

    e1, r1 = exp_rows(jnp.where(first, qn, 0.0))
    e2, r2 = exp_rows(jnp.where(first, 0.0, qn))
    p = e1 * r1 + e2 * (-lam * r2)
    o = _dot(p.astype(BF16), v_ref[0])
    o = o * lax.rsqrt(jnp.mean(o * o, axis=-1, keepdims=True) + EPS) * sg_ref[...] * (1.0 - lam_init)
    o_ref[0] = o.astype(BF16)


def _diff_attn_call(qkv, tiles_a, qk_g, lam_p, subln_g, lam_init):
    bsz, s, _ = qkv.shape
    tq = min(256, s)
    dup = lambda g: jnp.concatenate([g, g]).reshape(1, LANES)
    return pl.pallas_call(
        functools.partial(_diff_attn_kernel, lam_init=lam_init),
        grid=(bsz, A_HEADS, s // tq),
        in_specs=[pl.BlockSpec((1, tq, LANES), lambda b, h, i: (b, i, h)),
                  pl.BlockSpec((1, s, LANES), lambda b, h, i: (b, 0, A_HEADS + h)),
                  pl.BlockSpec((1, s, LANES), lambda b, h, i: (b, 0, 2 * A_HEADS + h)),
                  pl.BlockSpec((1, 5, LANES, LANES), lambda b, h, i: (h, 0, 0, 0)),
                  pl.BlockSpec((1, LANES), lambda b, h, i: (0, 0)),
                  pl.BlockSpec((1, LANES), lambda b, h, i: (0, 0)),
                  pl.BlockSpec(lam_p.shape, lambda b, h, i: (0, 0)),
                  pl.BlockSpec((1, LANES), lambda b, h, i: (0, 0))],
        out_specs=pl.BlockSpec((1, tq, LANES), lambda b, h, i: (b, i, h)),
        out_shape=jax.ShapeDtypeStruct((bsz, s, A_HEADS * LANES), BF16),
        scratch_shapes=[pltpu.VMEM((s, LANES), BF16), pltpu.VMEM((tq, s), F32)],
        compiler_params=_params("arbitrary", "arbitrary", "arbitrary"),
        name="diff_attn",
    )(qkv, qkv, qkv, tiles_a * LOG2E, dup(qk_g[0]), dup(qk_g[1]), lam_p, subln_g.reshape(1, LANES))


def _win_attn_kernel(q_ref, k_ref, v_ref, bias_ref, qg_ref, kg_ref, sink_ref, o_ref, kn_scr, *, q_blocks):
    i = pl.program_id(1)
    nb = pl.num_programs(1) * q_blocks
    group = B_Q_HEADS // B_KV_HEADS
    first = lax.broadcasted_iota(jnp.int32, (1, LANES), 1) < HEAD_DIM

    @pl.when(i == 0)
    def _():
        kn_scr[...] = _norm_halves(k_ref[0].astype(F32), kg_ref[...], first).astype(BF16)

    r = lax.broadcasted_iota(jnp.int32, (group * LANES, 1), 0) & (LANES - 1)
    c = lax.broadcasted_iota(jnp.int32, (1, 3 * LANES), 1)
    band = jnp.abs(c - LANES - r) <= WINDOW
    sinks = [jnp.max(sink_ref[kv], axis=-1, keepdims=True) for kv in range(B_KV_HEADS)]

    for u in range(q_blocks):
        blk = i * q_blocks + u
        rows = slice(u * LANES, (u + 1) * LANES)
        starts = [pl.multiple_of(j * LANES, LANES)
                  for j in (jnp.maximum(blk - 1, 0), blk, jnp.minimum(blk + 1, nb - 1))]
        k3 = jnp.concatenate([kn_scr[pl.ds(st, LANES), :] for st in starts], axis=0)
        v3 = jnp.concatenate([v_ref[0, pl.ds(st, LANES), :] for st in starts], axis=0)
        valid = band & ((c >= LANES) | (blk > 0)) & ((c < 2 * LANES) | (blk < nb - 1))

        pairs = []
        for pb in range(B_Q_HEADS // 2):
            qp = _norm_halves(q_ref[0, rows, pb * LANES:(pb + 1) * LANES].astype(F32), qg_ref[...], first)
            qp = qp * (HEAD_DIM ** -0.5 * LOG2E)
            pairs.append((qp, pltpu.roll(qp, HEAD_DIM, 1)))

        for kv in range(B_KV_HEADS):
            kv_lanes = first if kv == 0 else jnp.logical_not(first)
            pbs = range(kv * group // 2, (kv + 1) * group // 2)
            qs = jnp.concatenate([jnp.where(kv_lanes, pairs[pb][0 if a == kv else 1], 0.0).astype(BF16)
                                  for pb in pbs for a in range(2)], axis=0)
            sc = lax.dot_general(qs, k3, _TRANS_B, preferred_element_type=F32) + bias_ref[kv]
            sc = jnp.where(valid, sc, -jnp.inf)
            sk = sinks[kv]
            m = jnp.maximum(jnp.max(sc, axis=-1, keepdims=True), sk)
            e = jnp.exp2(sc - m)
            p = e * (1.0 / (jnp.sum(e, axis=-1, keepdims=True) + jnp.exp2(sk - m)))
            o = _dot(p.astype(BF16), v3)
            for n, pb in enumerate(pbs):
                halves = []
                for a in range(2):
                    oh = o[(2 * n + a) * LANES:(2 * n + a + 1) * LANES]
                    halves.append(oh if a == kv else pltpu.roll(oh, HEAD_DIM, 1))
                o_ref[0, rows, pb * LANES:(pb + 1) * LANES] = jnp.where(first, halves[0], halves[1]).astype(BF16)


def _win_attn_call(qkv, tiles_b, qk_g, sink):
    bsz, s, n = qkv.shape
    qw = B_Q_HEADS * HEAD_DIM
    group = B_Q_HEADS // B_KV_HEADS
    q_blocks = min(4, s // LANES)
    tq = q_blocks * LANES
    q_blk = (2 * A_HEADS * LANES + A_HEADS * LANES) // qw
    k_blk = (2 * A_HEADS * LANES + A_HEADS * LANES + qw) // LANES
    dup = lambda g: jnp.concatenate([g, g]).reshape(1, LANES)
    bias = tiles_b.reshape(B_KV_HEADS, group * LANES, 3 * LANES) * LOG2E
    sink_rows = jnp.broadcast_to(jnp.repeat(sink.astype(F32).reshape(B_KV_HEADS, group) * LOG2E, LANES, axis=1)[..., None],
                                 (B_KV_HEADS, group * LANES, LANES))
    return pl.pallas_call(
        functools.partial(_win_attn_kernel, q_blocks=q_blocks),
        grid=(bsz, s // tq),
        in_specs=[pl.BlockSpec((1, tq, qw), lambda b, i: (b, i, q_blk)),
                  pl.BlockSpec((1, s, LANES), lambda b, i: (b, 0, k_blk)),
                  pl.BlockSpec((1, s, LANES), lambda b, i: (b, 0, k_blk + 1)),
                  pl.BlockSpec(bias.shape, lambda b, i: (0, 0, 0)),
                  pl.BlockSpec((1, LANES), lambda b, i: (0, 0)),
                  pl.BlockSpec((1, LANES), lambda b, i: (0, 0)),
                  pl.BlockSpec(sink_rows.shape, lambda b, i: (0, 0, 0))],
        out_specs=pl.BlockSpec((1, tq, qw), lambda b, i: (b, i, 0)),
        out_shape=jax.ShapeDtypeStruct((bsz, s, qw), BF16),
        scratch_shapes=[pltpu.VMEM((s, LANES), BF16)],
        compiler_params=_params("arbitrary", "arbitrary"),
        name="win_attn",
    )(qkv, qkv, qkv, bias, dup(qk_g[2]), dup(qk_g[3]), sink_rows)


def _gla_constants(cs):
    t = np.arange(cs)[:, None]
    u = np.arange(cs)[None, :]
    sums, pairs = [], []
    for rev in (False, True):
        mats = [(u >= t) if rev else (u <= t)]
        for h in (1, 2, 4):
            start = (t // (2 * h)) * 2 * h
            later = (t // h) % 2 == 1
            if rev:
                b = start + h
                mats.append(np.where(later, (u >= b) & (u < t), (u >= t) & (u < b)))
            else:
                b = start + h - 1
                mats.append(np.where(later, (u > b) & (u <= t), (u > t) & (u <= b)))
        sums.append(np.concatenate(mats, axis=0))
        lv, h = [], 1
        while h < cs:
            same = (t // (2 * h)) == (u // (2 * h))
            t_later = (t // h) % 2 == 1
            u_later = (u // h) % 2 == 1
            lv.append(same & (t_later != rev) & (u_later == rev))
            h *= 2
        pairs.append(np.stack(lv))
    return jnp.asarray(np.stack(sums), BF16), jnp.asarray(np.stack(pairs), F32)


def _gla_chunks(chains, pairs_ref):
    cs = chains[0][0].shape[0]
    ex_alls = []
    for q, k, v, lf2, st, sums, rev in chains:
        hi = lf2.astype(BF16)
        lo = (lf2 - hi.astype(F32)).astype(BF16)
        ex_alls.append(_dot(sums, hi) + _dot(sums, lo))

    outs, states, gcs = [], [], []
    for (q, k, v, lf2, st, sums, rev), ex_all in zip(chains, ex_alls):
        gc = ex_all[0:cs]
        tot = gc[0:1] if rev else gc[cs - 1:cs]
        o = lax.dot_general((q * jnp.exp2(gc)).astype(BF16), st.astype(BF16), _TRANS_B, preferred_element_type=F32)
        kd = (k * jnp.exp2(tot - gc)).astype(BF16)
        states.append(jnp.exp2(tot) * st + lax.dot_general(v, kd, _TRANS_A, preferred_element_type=F32))
        outs.append(o + jnp.sum(q * k, axis=-1, keepdims=True) * v.astype(F32))
        gcs.append(gc)

    attns = [jnp.zeros((cs, cs), F32) for _ in chains]
    for level in range(cs.bit_length() - 1):
        h = 1 << level
        for n, ((q, k, v, lf2, st, sums, rev), ex_all, gc) in enumerate(zip(chains, ex_alls, gcs)):
            if h < SUBLANES:
                ex = ex_all[(level + 1) * cs:(level + 2) * cs]
            else:
                bounds = [p * 2 * h + (h if rev else h - 1) for p in range(cs // (2 * h))]
                ref = jnp.concatenate([jnp.broadcast_to(gc[b:b + 1], (2 * h, gc.shape[1])) for b in bounds], axis=0)
                d = gc - ref
                ex = jnp.minimum(d, -d)
            x = jnp.exp2(ex)
            pr = lax.dot_general((q * x).astype(BF16), (k * x).astype(BF16), _TRANS_B, preferred_element_type=F32)
            attns[n] = attns[n] + pr * pairs_ref[int(rev), level]
    return [(o + _dot(attn.astype(BF16), ch[2]), st) for o, attn, ch, st in zip(outs, attns, chains, states)]


def _gla_kernel(qs_ref, lff_ref, lfb_ref, kf_ref, kb_ref, v_ref, gs_ref, og_ref, sums_ref, pairs_ref, y_ref,
                of_scr, ob_scr):
    s_len = qs_ref.shape[1]
    heads = qs_ref.shape[2] // LANES
    cs = sums_ref.shape[2]
    nc = s_len // cs
    specs = [(j, rev, lf_ref, k_ref, o_scr) for j in range(heads)
             for rev, lf_ref, k_ref, o_scr in ((False, lff_ref, kf_ref, of_scr), (True, lfb_ref, kb_ref, ob_scr))]

    def body(c, states):
        where = [(pl.ds(pl.multiple_of(((nc - 1 - c) if rev else c) * cs, cs), cs), slice(j * LANES, (j + 1) * LANES))
                 for j, rev, *_ in specs]
        chains = [(qs_ref[0, r, l].astype(F32), k_ref[0, r, l].astype(F32), v_ref[0, r, l], lf_ref[0, r, l],
                   st, sums_ref[int(rev)], rev)
                  for (r, l), st, (j, rev, lf_ref, k_ref, _) in zip(where, states, specs)]
        outs = _gla_chunks(chains, pairs_ref)
        for (r, l), (o, _), (*_, o_scr) in zip(where, outs, specs):
            o_scr[r, l] = o
        return tuple(st for _, st in outs)

    zero = jnp.zeros((LANES, LANES), F32)
    lax.fori_loop(0, nc, body, (zero,) * len(specs))
    for j in range(heads):
        l = slice(j * LANES, (j + 1) * LANES)
        o = of_scr[:, l] + ob_scr[:, l]
        y = o * lax.rsqrt(jnp.mean(o * o, axis=-1, keepdims=True) + EPS) * og_ref[...]
        y_ref[0, :, l] = (y * gs_ref[0, :, l].astype(F32)).astype(BF16)


def _gla_call(qs, lff, lfb, kf, kb, v, gs, out_g):
    bsz, s, hk = qs.shape
    sums, pairs = _gla_constants(min(GLA_CHUNK, s))
    width = GLA_HEADS_PER_STEP * LANES
    blk = pl.BlockSpec((1, s, width), lambda b, h: (b, 0, h))
    return pl.pallas_call(
        _gla_kernel,
        grid=(bsz, hk // width),
        in_specs=[blk] * 7 + [pl.BlockSpec((1, LANES), lambda b, h: (0, 0)),
                              pl.BlockSpec(sums.shape, lambda b, h: (0, 0, 0)),
                              pl.BlockSpec(pairs.shape, lambda b, h: (0, 0, 0, 0))],
        out_specs=blk,
        out_shape=jax.ShapeDtypeStruct((bsz, s, hk), BF16),
        scratch_shapes=[pltpu.VMEM((s, width), F32), pltpu.VMEM((s, width), F32)],
        compiler_params=_params("arbitrary", "arbitrary"),
        name="gla",
    )(qs, lff, lfb, kf, kb, v, gs, out_g.reshape(1, LANES), sums, pairs)


def kernel(x, c, ada_w, ada_b, norm_g, ffn_up, ffn_down, even_w_in, even_w_out, qk_norm_g, diff_lambda,
           diff_subln_g, sink_logit, rel_bias, odd_w_in, odd_w_out, c_lower_bound, c_out_norm_g):
    bsz, s, d = x.shape
    depth = ada_w.shape[0]
    assert d == 2 * A_HEADS * LANES == B_Q_HEADS * LANES == C_HEADS * LANES
    assert s % 256 == 0 or s == LANES

    mod_all = _mod_call(c, ada_w, ada_b)
    tiles_a, tiles_b = _bias_tiles(rel_bias)
    for l in range(depth):
        mod = mod_all[l].reshape(bsz, N_MOD, 1, d)
        sh1, sc1, g1, sh2, sc2, g2, sh3, sc3, g3 = (mod[:, k] for k in range(N_MOD))
        x = _ffn_call(x, norm_g[l, 0], sh1, sc1, g1, ffn_up[l, 0].astype(BF16), ffn_down[l, 0].astype(BF16))
        if l % 2 == 0:
            e = l // 2
            lam_init = 0.8 - 0.6 * math.exp(-0.3 * l)
            qkv = _even_in_call(x, norm_g[l, 1], sh2, sc2, even_w_in[e].astype(BF16))
            ya = _diff_attn_call(qkv, tiles_a, qk_norm_g[e], diff_lambda[e].astype(F32), diff_subln_g[e], lam_init)
            yb = _win_attn_call(qkv, tiles_b, qk_norm_g[e], sink_logit[e])
            x = _out_call(x, g2, [ya, yb], even_w_out[e].astype(BF16))
        else:
            o = l // 2
            parts = _odd_in_call(x, norm_g[l, 1], sh2, sc2, odd_w_in[o].astype(BF16),
                                 c_lower_bound.astype(F32), l)
            y = _gla_call(*parts, c_out_norm_g[o])
            x = _out_call(x, g2, [y], odd_w_out[o].astype(BF16))
        x = _ffn_call(x, norm_g[l, 2], sh3, sc3, g3, ffn_up[l, 1].astype(BF16), ffn_down[l, 1].astype(BF16))
    return x
```

```python
import functools
import math

import jax
import jax.numpy as jnp
import numpy as np
from jax import lax
from jax.experimental import pallas as pl
from jax.experimental.pallas import tpu as pltpu

F32 = jnp.float32
BF16 = jnp.bfloat16

EPS = 1e-6
N_MOD = 9
HEAD_DIM = 64
LANES = 128
A_HEADS = 4
B_Q_HEADS = 8
B_KV_HEADS = 2
WINDOW = 128
NUM_BUCKETS = 32
MAX_DISTANCE = 128
C_HEADS = 8
FFN_CHUNK = 1024
FFN_ROWS = 1024
SUBLANES = 8
GLA_CHUNK = 128
GLA_HEADS_PER_STEP = 2
LOG2E = math.log2(math.e)
V7X_VMEM_LIMIT_BYTES = 56 * 1024 * 1024

_TRANS_B = (((1,), (1,)), ((), ()))
_TRANS_A = (((0,), (0,)), ((), ()))


def _params(*sem):
    return pltpu.CompilerParams(dimension_semantics=sem, vmem_limit_bytes=V7X_VMEM_LIMIT_BYTES)


def _dot(a, b):
    return jnp.dot(a, b, preferred_element_type=F32)


def _silu(t):
    return t / (1.0 + jnp.exp(-t))


def _norm_mod(x, g, shift, scale):
    ms = jnp.mean(x * x, axis=-1, keepdims=True)
    y = x * lax.rsqrt(ms + EPS) * g
    return y * (1.0 + scale) + shift


def _norm_halves(x, g, first):
    sq = x * x
    s0 = jnp.sum(jnp.where(first, sq, 0.0), axis=-1, keepdims=True)
    s1 = jnp.sum(jnp.where(first, 0.0, sq), axis=-1, keepdims=True)
    ms = jnp.where(first, s0, s1) * (1.0 / HEAD_DIM)
    return x * lax.rsqrt(ms + EPS) * g


def _mod_kernel(c_ref, w_ref, b_ref, o_ref):
    sc = _silu(c_ref[...])
    o_ref[0] = jnp.dot(sc, w_ref[0], precision=lax.Precision.HIGHEST,
                       preferred_element_type=F32) + b_ref[0]


def _mod_call(c, ada_w, ada_b):
    depth, d, m = ada_w.shape
    bsz = c.shape[0]
    tn = m // 8
    return pl.pallas_call(
        _mod_kernel,
        grid=(depth, m // tn),
        in_specs=[pl.BlockSpec((bsz, d), lambda l, j: (0, 0)),
                  pl.BlockSpec((1, d, tn), lambda l, j: (l, 0, j)),
                  pl.BlockSpec((1, 1, tn), lambda l, j: (l, 0, j))],
        out_specs=pl.BlockSpec((1, bsz, tn), lambda l, j: (l, 0, j)),
        out_shape=jax.ShapeDtypeStruct((depth, bsz, m), F32),
        compiler_params=_params("arbitrary", "arbitrary"),
        name="adaln_mod",
    )(c, ada_w, ada_b.reshape(depth, 1, m))


def _ffn_kernel(x_ref, g_ref, sh_ref, sc_ref, gate_ref, wu_ref, wd_ref, *rest, chunks):
    o_ref = rest[-1]
    x = x_ref[0]
    if len(rest) > 1:
        n_in = (len(rest) - 2) // 2
        mix = _dot(rest[1][0], rest[1 + n_in][...])
        for y_ref, w_ref in zip(rest[2:1 + n_in], rest[2 + n_in:1 + 2 * n_in]):
            mix += _dot(y_ref[0], w_ref[...])
        x = x + rest[0][0] * mix
    h = _norm_mod(x, g_ref[...], sh_ref[0], sc_ref[0]).astype(BF16)
    f = wd_ref.shape[0]
    acc = None
    for c0, cw in chunks:
        a = _dot(h, wu_ref[:, c0:c0 + cw])
        b = _dot(h, wu_ref[:, f + c0:f + c0 + cw])
        part = _dot((_silu(a) * b).astype(BF16), wd_ref[c0:c0 + cw, :])
        acc = part if acc is None else acc + part
    o_ref[0] = x + 0.5 * gate_ref[0] * acc


def _ffn_call(x, g, shift, scale, gate, w_up, w_down, mixer=None):
    bsz, s, d = x.shape
    f = w_down.shape[0]
    tm = min(FFN_ROWS, s)
    chunks = tuple((c0, min(FFN_CHUNK, f - c0)) for c0 in range(0, f, FFN_CHUNK))
    row = lambda b, i: (b, i, 0)
    vec = lambda b, i: (b, 0, 0)
    resident = dict(pipeline_mode=pl.Buffered(1))
    in_specs = [pl.BlockSpec((1, tm, d), row),
                pl.BlockSpec((1, d), lambda b, i: (0, 0)),
                pl.BlockSpec((1, 1, d), vec),
                pl.BlockSpec((1, 1, d), vec),
                pl.BlockSpec((1, 1, d), vec),
                pl.BlockSpec(w_up.shape, lambda b, i: (0, 0), **resident),
                pl.BlockSpec(w_down.shape, lambda b, i: (0, 0), **resident)]
    args = [x, g.reshape(1, d), shift, scale, gate, w_up, w_down]
    if mixer is not None:
        mixer_gate, ys, w_out = mixer
        wk = ys[0].shape[-1]
        in_specs += ([pl.BlockSpec((1, 1, d), vec)] + [pl.BlockSpec((1, tm, wk), row)] * len(ys)
                     + [pl.BlockSpec((wk, d), functools.partial(lambda b, i, k: (k, 0), k=k), **resident)
                        for k in range(len(ys))])
        args += [mixer_gate, *ys, *([w_out] * len(ys))]
    return pl.pallas_call(
        functools.partial(_ffn_kernel, chunks=chunks),
        grid=(bsz, s // tm),
        in_specs=in_specs,
        out_specs=pl.BlockSpec((1, tm, d), row),
        out_shape=jax.ShapeDtypeStruct(x.shape, F32),
        compiler_params=_params("arbitrary", "arbitrary"),
        name="ffn",
    )(*args)


def _even_in_kernel(x_ref, g_ref, sh_ref, sc_ref, w_ref, o_ref):
    h = _norm_mod(x_ref[0], g_ref[...], sh_ref[0], sc_ref[0]).astype(BF16)
    o_ref[0] = _dot(h, w_ref[...]).astype(BF16)


def _even_in_call(x, g, shift, scale, w):
    bsz, s, d = x.shape
    n = w.shape[1]
    tm = min(512, s)
    row = lambda b, i: (b, i, 0)
    vec = lambda b, i: (b, 0, 0)
    return pl.pallas_call(
        _even_in_kernel,
        grid=(bsz, s // tm),
        in_specs=[pl.BlockSpec((1, tm, d), row),
                  pl.BlockSpec((1, d), lambda b, i: (0, 0)),
                  pl.BlockSpec((1, 1, d), vec),
                  pl.BlockSpec((1, 1, d), vec),
                  pl.BlockSpec((d, n), lambda b, i: (0, 0))],
        out_specs=pl.BlockSpec((1, tm, n), row),
        out_shape=jax.ShapeDtypeStruct((bsz, s, n), BF16),
        compiler_params=_params("arbitrary", "arbitrary"),
        name="even_in_proj",
    )(x, g.reshape(1, d), shift, scale, w)


def _log_forget_and_input_gate(f_raw, clb, layer):
    e = jnp.exp(clb - jnp.max(clb, axis=0, keepdims=True))
    lb = jnp.sum(e[1:layer + 1], axis=0, keepdims=True) / jnp.sum(e, axis=0, keepdims=True)
    t = jnp.exp(-jnp.abs(f_raw))
    r = 1.0 / (1.0 + t)
    pos = f_raw >= 0.0
    f = lb + (1.0 - lb) * (jnp.where(pos, 1.0, t) * r)
    lf = jnp.where(f > 0.0, jnp.log(f), jnp.log(1.0 - lb) + f_raw)
    return lf * LOG2E, (1.0 - lb) * (jnp.where(pos, t, 1.0) * r)


def _odd_in_kernel(x_ref, g_ref, sh_ref, sc_ref, w_ref, clb_ref,
                   qs_ref, lff_ref, lfb_ref, kf_ref, kb_ref, v_ref, gs_ref, *, layer):
    h = _norm_mod(x_ref[0], g_ref[...], sh_ref[0], sc_ref[0]).astype(BF16)
    hk = qs_ref.shape[-1]
    seg = lambda n: _dot(h, w_ref[:, n * hk:(n + 1) * hk])
    qs_ref[0] = _silu(seg(0)).astype(BF16)
    for n, lf_ref, k_ref in ((1, lff_ref, kf_ref), (2, lfb_ref, kb_ref)):
        lf2, k = _log_forget_and_input_gate(seg(n), clb_ref[n - 1], layer)
        lf_ref[0] = lf2
        k_ref[0] = k.astype(BF16)
    v_ref[0] = seg(3).astype(BF16)
    gs_ref[0] = _silu(seg(4)).astype(BF16)


def _odd_in_call(x, g, shift, scale, w, clb, layer):
    bsz, s, d = x.shape
    hk = clb.shape[-1]
    tm = min(512, s)
    row = lambda b, i: (b, i, 0)
    vec = lambda b, i: (b, 0, 0)
    out = lambda dt: jax.ShapeDtypeStruct((bsz, s, hk), dt)
    return pl.pallas_call(
        functools.partial(_odd_in_kernel, layer=layer),
        grid=(bsz, s // tm),
        in_specs=[pl.BlockSpec((1, tm, d), row),
                  pl.BlockSpec((1, d), lambda b, i: (0, 0)),
                  pl.BlockSpec((1, 1, d), vec),
                  pl.BlockSpec((1, 1, d), vec),
                  pl.BlockSpec(w.shape, lambda b, i: (0, 0), pipeline_mode=pl.Buffered(1)),
                  pl.BlockSpec(clb.shape, lambda b, i: (0, 0, 0))],
        out_specs=[pl.BlockSpec((1, tm, hk), row)] * 7,
        out_shape=[out(BF16), out(F32), out(F32), out(BF16), out(BF16), out(BF16), out(BF16)],
        compiler_params=_params("arbitrary", "arbitrary"),
        name="odd_in_proj",
    )(x, g.reshape(1, d), shift, scale, w, clb)


def _t5_bucket(rel):
    half = NUM_BUCKETS // 2
    max_exact = half // 2
    n = jnp.abs(rel)
    nf = jnp.maximum(n, 1).astype(F32)
    large = max_exact + (jnp.log(nf / max_exact) / math.log(MAX_DISTANCE / max_exact)
                         * (half - max_exact)).astype(jnp.int32)
    large = jnp.minimum(large, half - 1)
    return jnp.where(rel > 0, half, 0) + jnp.where(n < max_exact, n, large)


def _bias_tiles(rel_bias):
    r = jnp.arange(LANES)[:, None]
    c = jnp.arange(LANES)[None, :]

    def lookup(rel, table):
        hot = jax.nn.one_hot(_t5_bucket(rel), NUM_BUCKETS, dtype=F32)
        return jnp.einsum('...k,kh->h...', hot, table.astype(F32), precision=lax.Precision.HIGHEST)

    rel_a = (jnp.arange(-2, 3) * LANES)[:, None, None] + (c - r)[None]
    rel_b = jnp.arange(3 * LANES)[None, :] - LANES - r
    return lookup(rel_a, rel_bias[:, :A_HEADS]), lookup(rel_b, rel_bias[:, A_HEADS:])


def _diff_attn_kernel(q_ref, k_ref, v_ref, bias_ref, qg_ref, kg_ref, lam_ref, sg_ref, o_ref, kn_scr, *, lam_init):
    i = pl.program_id(2)
    tq = q_ref.shape[1]
    s_len = k_ref.shape[1]
    first = lax.broadcasted_iota(jnp.int32, (1, LANES), 1) < HEAD_DIM

    @pl.when(i == 0)
    def _():
        kn_scr[...] = _norm_halves(k_ref[0].astype(F32), kg_ref[...], first).astype(BF16)

    bias = jnp.concatenate([jnp.concatenate([bias_ref[0, jnp.clip(j - (i * (tq // LANES) + a), -2, 2) + 2]
                                             for j in range(s_len // LANES)], axis=1)
                            for a in range(tq // LANES)], axis=0)

    lp = lam_ref[...]
    lam = (jnp.exp(jnp.sum(lp[0:1] * lp[1:2], axis=-1, keepdims=True))
           - jnp.exp(jnp.sum(lp[2:3] * lp[3:4], axis=-1, keepdims=True)) + lam_init)

    qn = _norm_halves(q_ref[0].astype(F32), qg_ref[...], first) * (HEAD_DIM ** -0.5 * LOG2E)
    kn = kn_scr[...]
    qms = [jnp.where(first, qn, 0.0).astype(BF16), jnp.where(first, 0.0, qn).astype(BF16)]
    scs = [lax.dot_general(qm, kn, _TRANS_B, preferred_element_type=F32) + bias for qm in qms]
    es = [jnp.exp2(sc - jnp.max(sc, axis=-1, keepdims=True)) for sc in scs]
    rs = [1.0 / jnp.sum(e, axis=-1, keepdims=True) for e in es]
    evs = [_dot(e.astype(BF16), v_ref[0]) for e in es]
    o = evs[0] * rs[0] - evs[1] * (lam * rs[1])
    o = o * lax.rsqrt(jnp.mean(o * o, axis=-1, keepdims=True) + EPS) * sg_ref[...] * (1.0 - lam_init)
    o_ref[0] = o.astype(BF16)


def _diff_attn_call(qkv, tiles_a, qk_g, lam_p, subln_g, lam_init):
    bsz, s, _ = qkv.shape
    tq = min(256, s)
    dup = lambda g: jnp.concatenate([g, g]).reshape(1, LANES)
    return pl.pallas_call(
        functools.partial(_diff_attn_kernel, lam_init=lam_init),
        grid=(bsz, A_HEADS, s // tq),
        in_specs=[pl.BlockSpec((1, tq, LANES), lambda b, h, i: (b, i, h)),
                  pl.BlockSpec((1, s, LANES), lambda b, h, i: (b, 0, A_HEADS + h)),
                  pl.BlockSpec((1, s, LANES), lambda b, h, i: (b, 0, 2 * A_HEADS + h)),
                  pl.BlockSpec((1, 5, LANES, LANES), lambda b, h, i: (h, 0, 0, 0)),
                  pl.BlockSpec((1, LANES), lambda b, h, i: (0, 0)),
                  pl.BlockSpec((1, LANES), lambda b, h, i: (0, 0)),
                  pl.BlockSpec(lam_p.shape, lambda b, h, i: (0, 0)),
                  pl.BlockSpec((1, LANES), lambda b, h, i: (0, 0))],
        out_specs=pl.BlockSpec((1, tq, LANES), lambda b, h, i: (b, i, h)),
        out_shape=jax.ShapeDtypeStruct((bsz, s, A_HEADS * LANES), BF16),
        scratch_shapes=[pltpu.VMEM((s, LANES), BF16)],
        compiler_params=_params("arbitrary", "arbitrary", "arbitrary"),
        name="diff_attn",
    )(qkv, qkv, qkv, tiles_a * LOG2E, dup(qk_g[0]), dup(qk_g[1]), lam_p, subln_g.reshape(1, LANES))


def _win_attn_kernel(q_ref, k_ref, v_ref, bias_ref, qg_ref, kg_ref, sink_ref, o_ref, kn_scr, *, q_blocks):
    i = pl.program_id(1)
    nb = pl.num_programs(1) * q_blocks
    group = B_Q_HEADS // B_KV_HEADS
    first = lax.broadcasted_iota(jnp.int32, (1, LANES), 1) < HEAD_DIM

    @pl.when(i == 0)
    def _():
        kn_scr[...] = _norm_halves(k_ref[0].astype(F32), kg_ref[...], first).astype(BF16)

    r = lax.broadcasted_iota(jnp.int32, (group * LANES, 1), 0) & (LANES - 1)
    c = lax.broadcasted_iota(jnp.int32, (1, 3 * LANES), 1)
    band = jnp.abs(c - LANES - r) <= WINDOW
    sinks = [jnp.max(sink_ref[kv], axis=-1, keepdims=True) for kv in range(B_KV_HEADS)]

    for u in range(q_blocks):
        blk = i * q_blocks + u
        rows = slice(u * LANES, (u + 1) * LANES)
        starts = [pl.multiple_of(j * LANES, LANES)
                  for j in (jnp.maximum(blk - 1, 0), blk, jnp.minimum(blk + 1, nb - 1))]
        k3 = jnp.concatenate([kn_scr[pl.ds(st, LANES), :] for st in starts], axis=0)
        v3 = jnp.concatenate([v_ref[0, pl.ds(st, LANES), :] for st in starts], axis=0)
        valid = band & ((c >= LANES) | (blk > 0)) & ((c < 2 * LANES) | (blk < nb - 1))

        pairs = []
        for pb in range(B_Q_HEADS // 2):
            qp = _norm_halves(q_ref[0, rows, pb * LANES:(pb + 1) * LANES].astype(F32), qg_ref[...], first)
            qp = qp * (HEAD_DIM ** -0.5 * LOG2E)
            pairs.append((qp, pltpu.roll(qp, HEAD_DIM, 1)))

        for kv in range(B_KV_HEADS):
            kv_lanes = first if kv == 0 else jnp.logical_not(first)
            pbs = range(kv * group // 2, (kv + 1) * group // 2)
            qs = jnp.concatenate([jnp.where(kv_lanes, pairs[pb][0 if a == kv else 1], 0.0).astype(BF16)
                                  for pb in pbs for a in range(2)], axis=0)
            sc = lax.dot_general(qs, k3, _TRANS_B, preferred_element_type=F32) + bias_ref[kv]
            sc = jnp.where(valid, sc, -jnp.inf)
            sk = sinks[kv]
            m = jnp.maximum(jnp.max(sc, axis=-1, keepdims=True), sk)
            e = jnp.exp2(sc - m)
            p = e * (1.0 / (jnp.sum(e, axis=-1, keepdims=True) + jnp.exp2(sk - m)))
            o = _dot(p.astype(BF16), v3)
            for n, pb in enumerate(pbs):
                halves = []
                for a in range(2):
                    oh = o[(2 * n + a) * LANES:(2 * n + a + 1) * LANES]
                    halves.append(oh if a == kv else pltpu.roll(oh, HEAD_DIM, 1))
                o_ref[0, rows, pb * LANES:(pb + 1) * LANES] = jnp.where(first, halves[0], halves[1]).astype(BF16)


def _win_attn_call(qkv, tiles_b, qk_g, sink):
    bsz, s, n = qkv.shape
    qw = B_Q_HEADS * HEAD_DIM
    group = B_Q_HEADS // B_KV_HEADS
    q_blocks = min(4, s // LANES)
    tq = q_blocks * LANES
    q_blk = (2 * A_HEADS * LANES + A_HEADS * LANES) // qw
    k_blk = (2 * A_HEADS * LANES + A_HEADS * LANES + qw) // LANES
    dup = lambda g: jnp.concatenate([g, g]).reshape(1, LANES)
    bias = tiles_b.reshape(B_KV_HEADS, group * LANES, 3 * LANES) * LOG2E
    sink_rows = jnp.broadcast_to(jnp.repeat(sink.astype(F32).reshape(B_KV_HEADS, group) * LOG2E, LANES, axis=1)[..., None],
                                 (B_KV_HEADS, group * LANES, LANES))
    return pl.pallas_call(
        functools.partial(_win_attn_kernel, q_blocks=q_blocks),
        grid=(bsz, s // tq),
        in_specs=[pl.BlockSpec((1, tq, qw), lambda b, i: (b, i, q_blk)),
                  pl.BlockSpec((1, s, LANES), lambda b, i: (b, 0, k_blk)),
                  pl.BlockSpec((1, s, LANES), lambda b, i: (b, 0, k_blk + 1)),
                  pl.BlockSpec(bias.shape, lambda b, i: (0, 0, 0)),
                  pl.BlockSpec((1, LANES), lambda b, i: (0, 0)),
                  pl.BlockSpec((1, LANES), lambda b, i: (0, 0)),
                  pl.BlockSpec(sink_rows.shape, lambda b, i: (0, 0, 0))],
        out_specs=pl.BlockSpec((1, tq, qw), lambda b, i: (b, i, 0)),
        out_shape=jax.ShapeDtypeStruct((bsz, s, qw), BF16),
        scratch_shapes=[pltpu.VMEM((s, LANES), BF16)],
        compiler_params=_params("arbitrary", "arbitrary"),
        name="win_attn",
    )(qkv, qkv, qkv, bias, dup(qk_g[2]), dup(qk_g[3]), sink_rows)


def _gla_constants(cs):
    t = np.arange(cs)[:, None]
    u = np.arange(cs)[None, :]
    sums, pairs = [], []
    for rev in (False, True):
        mats = [(u >= t) if rev else (u <= t)]
        for h in (1, 2, 4):
            start = (t // (2 * h)) * 2 * h
            later = (t // h) % 2 == 1
            if rev:
                b = start + h
                mats.append(np.where(later, (u >= b) & (u < t), (u >= t) & (u < b)))
            else:
                b = start + h - 1
                mats.append(np.where(later, (u > b) & (u <= t), (u > t) & (u <= b)))
        sums.append(np.concatenate(mats, axis=0))
        lv, h = [], 1
        while h < cs:
            same = (t // (2 * h)) == (u // (2 * h))
            t_later = (t // h) % 2 == 1
            u_later = (u // h) % 2 == 1
            lv.append(same & (t_later != rev) & (u_later == rev))
            h *= 2
        pairs.append(np.stack(lv))
    return jnp.asarray(np.stack(sums), BF16), jnp.asarray(np.stack(pairs), F32)


def _gla_chunks(chains, pairs_ref):
    cs = chains[0][0].shape[0]
    ex_alls = []
    for q, k, v, lf2, st, sums, rev in chains:
        hi = lf2.astype(BF16)
        lo = (lf2 - hi.astype(F32)).astype(BF16)
        ex_alls.append(_dot(sums, hi) + _dot(sums, lo))

    outs, states, gcs = [], [], []
    for (q, k, v, lf2, st, sums, rev), ex_all in zip(chains, ex_alls):
        gc = ex_all[0:cs]
        tot = gc[0:1] if rev else gc[cs - 1:cs]
        o = lax.dot_general((q * jnp.exp2(gc)).astype(BF16), st.astype(BF16), _TRANS_B, preferred_element_type=F32)
        kd = (k * jnp.exp2(tot - gc)).astype(BF16)
        states.append(jnp.exp2(tot) * st + lax.dot_general(v, kd, _TRANS_A, preferred_element_type=F32))
        outs.append(o + jnp.sum(q * k, axis=-1, keepdims=True) * v.astype(F32))
        gcs.append(gc)

    attns = [jnp.zeros((cs, cs), F32) for _ in chains]
    for level in range(cs.bit_length() - 1):
        h = 1 << level
        for n, ((q, k, v, lf2, st, sums, rev), ex_all, gc) in enumerate(zip(chains, ex_alls, gcs)):
            if h < SUBLANES:
                ex = ex_all[(level + 1) * cs:(level + 2) * cs]
            else:
                bounds = [p * 2 * h + (h if rev else h - 1) for p in range(cs // (2 * h))]
                ref = jnp.concatenate([jnp.broadcast_to(gc[b:b + 1], (2 * h, gc.shape[1])) for b in bounds], axis=0)
                d = gc - ref
                ex = jnp.minimum(d, -d)
            x = jnp.exp2(ex)
            pr = lax.dot_general((q * x).astype(BF16), (k * x).astype(BF16), _TRANS_B, preferred_element_type=F32)
            attns[n] = attns[n] + pr * pairs_ref[int(rev), level]
    return [(o + _dot(attn.astype(BF16), ch[2]), st) for o, attn, ch, st in zip(outs, attns, chains, states)]


def _gla_kernel(qs_ref, lff_ref, lfb_ref, kf_ref, kb_ref, v_ref, gs_ref, og_ref, sums_ref, pairs_ref, y_ref,
                of_scr, ob_scr):
    s_len = qs_ref.shape[1]
    heads = qs_ref.shape[2] // LANES
    cs = sums_ref.shape[2]
    nc = s_len // cs
    specs = [(j, rev, lf_ref, k_ref, o_scr) for j in range(heads)
             for rev, lf_ref, k_ref, o_scr in ((False, lff_ref, kf_ref, of_scr), (True, lfb_ref, kb_ref, ob_scr))]

    def body(c, states):
        where = [(pl.ds(pl.multiple_of(((nc - 1 - c) if rev else c) * cs, cs), cs), slice(j * LANES, (j + 1) * LANES))
                 for j, rev, *_ in specs]
        chains = [(qs_ref[0, r, l].astype(F32), k_ref[0, r, l].astype(F32), v_ref[0, r, l], lf_ref[0, r, l],
                   st, sums_ref[int(rev)], rev)
                  for (r, l), st, (j, rev, lf_ref, k_ref, _) in zip(where, states, specs)]
        outs = _gla_chunks(chains, pairs_ref)
        for (r, l), (o, _), (*_, o_scr) in zip(where, outs, specs):
            o_scr[r, l] = o
        return tuple(st for _, st in outs)

    zero = jnp.zeros((LANES, LANES), F32)
    lax.fori_loop(0, nc, body, (zero,) * len(specs))
    for j in range(heads):
        l = slice(j * LANES, (j + 1) * LANES)
        o = of_scr[:, l] + ob_scr[:, l]
        y = o * lax.rsqrt(jnp.mean(o * o, axis=-1, keepdims=True) + EPS) * og_ref[...]
        y_ref[0, :, l] = (y * gs_ref[0, :, l].astype(F32)).astype(BF16)


def _gla_call(qs, lff, lfb, kf, kb, v, gs, out_g):
    bsz, s, hk = qs.shape
    sums, pairs = _gla_constants(min(GLA_CHUNK, s))
    width = GLA_HEADS_PER_STEP * LANES
    blk = pl.BlockSpec((1, s, width), lambda b, h: (b, 0, h))
    return pl.pallas_call(
        _gla_kernel,
        grid=(bsz, hk // width),
        in_specs=[blk] * 7 + [pl.BlockSpec((1, LANES), lambda b, h: (0, 0)),
                              pl.BlockSpec(sums.shape, lambda b, h: (0, 0, 0)),
                              pl.BlockSpec(pairs.shape, lambda b, h: (0, 0, 0, 0))],
        out_specs=blk,
        out_shape=jax.ShapeDtypeStruct((bsz, s, hk), BF16),
        scratch_shapes=[pltpu.VMEM((s, width), F32), pltpu.VMEM((s, width), F32)],
        compiler_params=_params("arbitrary", "arbitrary"),
        name="gla",
    )(qs, lff, lfb, kf, kb, v, gs, out_g.reshape(1, LANES), sums, pairs)


def kernel(x, c, ada_w, ada_b, norm_g, ffn_up, ffn_down, even_w_in, even_w_out, qk_norm_g, diff_lambda,
           diff_subln_g, sink_logit, rel_bias, odd_w_in, odd_w_out, c_lower_bound, c_out_norm_g):
    bsz, s, d = x.shape
    depth = ada_w.shape[0]
    assert d == 2 * A_HEADS * LANES == B_Q_HEADS * LANES == C_HEADS * LANES
    assert s % 256 == 0 or s == LANES

    mod_all = _mod_call(c, ada_w, ada_b)
    tiles_a, tiles_b = _bias_tiles(rel_bias)
    for l in range(depth):
        mod = mod_all[l].reshape(bsz, N_MOD, 1, d)
        sh1, sc1, g1, sh2, sc2, g2, sh3, sc3, g3 = (mod[:, k] for k in range(N_MOD))
        x = _ffn_call(x, norm_g[l, 0], sh1, sc1, g1, ffn_up[l, 0].astype(BF16), ffn_down[l, 0].astype(BF16))
        if l % 2 == 0:
            e = l // 2
            lam_init = 0.8 - 0.6 * math.exp(-0.3 * l)
            qkv = _even_in_call(x, norm_g[l, 1], sh2, sc2, even_w_in[e].astype(BF16))
            ya = _diff_attn_call(qkv, tiles_a, qk_norm_g[e], diff_lambda[e].astype(F32), diff_subln_g[e], lam_init)
            yb = _win_attn_call(qkv, tiles_b, qk_norm_g[e], sink_logit[e])
            mixer = (g2, [ya, yb], even_w_out[e].astype(BF16))
        else:
            o = l // 2
            parts = _odd_in_call(x, norm_g[l, 1], sh2, sc2, odd_w_in[o].astype(BF16),
                                 c_lower_bound.astype(F32), l)
            mixer = (g2, [_gla_call(*parts, c_out_norm_g[o])], odd_w_out[o].astype(BF16))
        x = _ffn_call(x, norm_g[l, 2], sh3, sc3, g3, ffn_up[l, 1].astype(BF16), ffn_down[l, 1].astype(BF16), mixer)
    return x
```

```python
import functools
import math

import jax
import jax.numpy as jnp
import numpy as np
from jax import lax
from jax.experimental import pallas as pl
from jax.experimental.pallas import tpu as pltpu

F32 = jnp.float32
BF16 = jnp.bfloat16

EPS = 1e-6
N_MOD = 9
HEAD_DIM = 64
LANES = 128
A_HEADS = 4
B_Q_HEADS = 8
B_KV_HEADS = 2
WINDOW = 128
NUM_BUCKETS = 32
MAX_DISTANCE = 128
C_HEADS = 8
FFN_CHUNK = 1024
FFN_ROWS = 1024
DIFF_ATTN_ROWS = 512
SUBLANES = 8
GLA_CHUNK = 128
GLA_HEADS_PER_STEP = 2
LOG2E = math.log2(math.e)
V7X_VMEM_LIMIT_BYTES = 56 * 1024 * 1024

_TRANS_B = (((1,), (1,)), ((), ()))
_TRANS_A = (((0,), (0,)), ((), ()))


def _params(*sem):
    return pltpu.CompilerParams(dimension_semantics=sem, vmem_limit_bytes=V7X_VMEM_LIMIT_BYTES)


def _dot(a, b):
    return jnp.dot(a, b, preferred_element_type=F32)


def _silu(t):
    u = 0.5 * t
    return u + u * jnp.tanh(u)


def _norm_mod(x, g, shift, scale):
    ms = jnp.mean(x * x, axis=-1, keepdims=True)
    y = x * lax.rsqrt(ms + EPS) * g
    return y * (1.0 + scale) + shift


def _norm_halves(x, g, first):
    sq = x * x
    s0 = jnp.sum(jnp.where(first, sq, 0.0), axis=-1, keepdims=True)
    s1 = jnp.sum(jnp.where(first, 0.0, sq), axis=-1, keepdims=True)
    ms = jnp.where(first, s0, s1) * (1.0 / HEAD_DIM)
    return x * lax.rsqrt(ms + EPS) * g


def _mod_kernel(c_ref, w_ref, b_ref, o_ref):
    sc = _silu(c_ref[...])
    o_ref[0] = jnp.dot(sc, w_ref[0], precision=lax.Precision.HIGHEST,
                       preferred_element_type=F32) + b_ref[0]


def _mod_call(c, ada_w, ada_b):
    depth, d, m = ada_w.shape
    bsz = c.shape[0]
    tn = m // 8
    return pl.pallas_call(
        _mod_kernel,
        grid=(depth, m // tn),
        in_specs=[pl.BlockSpec((bsz, d), lambda l, j: (0, 0)),
                  pl.BlockSpec((1, d, tn), lambda l, j: (l, 0, j)),
                  pl.BlockSpec((1, 1, tn), lambda l, j: (l, 0, j))],
        out_specs=pl.BlockSpec((1, bsz, tn), lambda l, j: (l, 0, j)),
        out_shape=jax.ShapeDtypeStruct((depth, bsz, m), F32),
        compiler_params=_params("arbitrary", "arbitrary"),
        name="adaln_mod",
    )(c, ada_w, ada_b.reshape(depth, 1, m))


def _ffn_kernel(x_ref, g_ref, sh_ref, sc_ref, gate_ref, wu_ref, wd_ref, *rest, chunks):
    o_ref = rest[-1]
    x = x_ref[0]
    if len(rest) > 1:
        n_in = (len(rest) - 2) // 2
        mix = _dot(rest[1][0], rest[1 + n_in][...])
        for y_ref, w_ref in zip(rest[2:1 + n_in], rest[2 + n_in:1 + 2 * n_in]):
            mix += _dot(y_ref[0], w_ref[...])
        x = x + rest[0][0] * mix
    h = _norm_mod(x, g_ref[...], sh_ref[0], sc_ref[0]).astype(BF16)
    f = wd_ref.shape[0]
    acc = None
    for c0, cw in chunks:
        a = _dot(h, wu_ref[:, c0:c0 + cw])
        b = _dot(h, wu_ref[:, f + c0:f + c0 + cw])
        part = _dot((_silu(a) * b).astype(BF16), wd_ref[c0:c0 + cw, :])
        acc = part if acc is None else acc + part
    o_ref[0] = x + 0.5 * gate_ref[0] * acc


def _ffn_call(x, g, shift, scale, gate, w_up, w_down, mixer=None):
    bsz, s, d = x.shape
    f = w_down.shape[0]
    tm = min(FFN_ROWS, s)
    chunks = tuple((c0, min(FFN_CHUNK, f - c0)) for c0 in range(0, f, FFN_CHUNK))
    row = lambda b, i: (b, i, 0)
    vec = lambda b, i: (b, 0, 0)
    resident = dict(pipeline_mode=pl.Buffered(1))
    in_specs = [pl.BlockSpec((1, tm, d), row),
                pl.BlockSpec((1, d), lambda b, i: (0, 0)),
                pl.BlockSpec((1, 1, d), vec),
                pl.BlockSpec((1, 1, d), vec),
                pl.BlockSpec((1, 1, d), vec),
                pl.BlockSpec(w_up.shape, lambda b, i: (0, 0), **resident),
                pl.BlockSpec(w_down.shape, lambda b, i: (0, 0), **resident)]
    args = [x, g.reshape(1, d), shift, scale, gate, w_up, w_down]
    if mixer is not None:
        mixer_gate, ys, w_out = mixer
        wk = ys[0].shape[-1]
        in_specs += ([pl.BlockSpec((1, 1, d), vec)] + [pl.BlockSpec((1, tm, wk), row)] * len(ys)
                     + [pl.BlockSpec((wk, d), functools.partial(lambda b, i, k: (k, 0), k=k), **resident)
                        for k in range(len(ys))])
        args += [mixer_gate, *ys, *([w_out] * len(ys))]
    return pl.pallas_call(
        functools.partial(_ffn_kernel, chunks=chunks),
        grid=(bsz, s // tm),
        in_specs=in_specs,
        out_specs=pl.BlockSpec((1, tm, d), row),
        out_shape=jax.ShapeDtypeStruct(x.shape, F32),
        compiler_params=_params("arbitrary", "arbitrary"),
        name="ffn",
    )(*args)


def _even_in_kernel(x_ref, g_ref, sh_ref, sc_ref, w_ref, o_ref):
    h = _norm_mod(x_ref[0], g_ref[...], sh_ref[0], sc_ref[0]).astype(BF16)
    o_ref[0] = _dot(h, w_ref[...]).astype(BF16)


def _even_in_call(x, g, shift, scale, w):
    bsz, s, d = x.shape
    n = w.shape[1]
    tm = min(512, s)
    row = lambda b, i: (b, i, 0)
    vec = lambda b, i: (b, 0, 0)
    return pl.pallas_call(
        _even_in_kernel,
        grid=(bsz, s // tm),
        in_specs=[pl.BlockSpec((1, tm, d), row),
                  pl.BlockSpec((1, d), lambda b, i: (0, 0)),
                  pl.BlockSpec((1, 1, d), vec),
                  pl.BlockSpec((1, 1, d), vec),
                  pl.BlockSpec((d, n), lambda b, i: (0, 0))],
        out_specs=pl.BlockSpec((1, tm, n), row),
        out_shape=jax.ShapeDtypeStruct((bsz, s, n), BF16),
        compiler_params=_params("arbitrary", "arbitrary"),
        name="even_in_proj",
    )(x, g.reshape(1, d), shift, scale, w)


def _log_forget_and_input_gate(f_raw, clb, layer):
    e = jnp.exp(clb - jnp.max(clb, axis=0, keepdims=True))
    lb = jnp.sum(e[1:layer + 1], axis=0, keepdims=True) / jnp.sum(e, axis=0, keepdims=True)
    half_th = 0.5 * jnp.tanh(0.5 * f_raw)
    f = lb + (1.0 - lb) * (0.5 + half_th)
    lf2 = jnp.where(f > 0.0, jnp.log2(f), (jnp.log(1.0 - lb) + f_raw) * LOG2E)
    return lf2, (1.0 - lb) * (0.5 - half_th)


def _odd_in_kernel(x_ref, g_ref, sh_ref, sc_ref, w_ref, clb_ref,
                   qs_ref, lff_ref, lfb_ref, kf_ref, kb_ref, v_ref, gs_ref, *, layer):
    h = _norm_mod(x_ref[0], g_ref[...], sh_ref[0], sc_ref[0]).astype(BF16)
    hk = qs_ref.shape[-1]
    seg = lambda n: _dot(h, w_ref[:, n * hk:(n + 1) * hk])
    qs_ref[0] = _silu(seg(0)).astype(BF16)
    for n, lf_ref, k_ref in ((1, lff_ref, kf_ref), (2, lfb_ref, kb_ref)):
        lf2, k = _log_forget_and_input_gate(seg(n), clb_ref[n - 1], layer)
        lf_ref[0] = lf2
        k_ref[0] = k.astype(BF16)
    v_ref[0] = seg(3).astype(BF16)
    gs_ref[0] = _silu(seg(4)).astype(BF16)


def _odd_in_call(x, g, shift, scale, w, clb, layer):
    bsz, s, d = x.shape
    hk = clb.shape[-1]
    tm = min(512, s)
    row = lambda b, i: (b, i, 0)
    vec = lambda b, i: (b, 0, 0)
    out = lambda dt: jax.ShapeDtypeStruct((bsz, s, hk), dt)
    return pl.pallas_call(
        functools.partial(_odd_in_kernel, layer=layer),
        grid=(bsz, s // tm),
        in_specs=[pl.BlockSpec((1, tm, d), row),
                  pl.BlockSpec((1, d), lambda b, i: (0, 0)),
                  pl.BlockSpec((1, 1, d), vec),
                  pl.BlockSpec((1, 1, d), vec),
                  pl.BlockSpec(w.shape, lambda b, i: (0, 0), pipeline_mode=pl.Buffered(1)),
                  pl.BlockSpec(clb.shape, lambda b, i: (0, 0, 0))],
        out_specs=[pl.BlockSpec((1, tm, hk), row)] * 7,
        out_shape=[out(BF16), out(F32), out(F32), out(BF16), out(BF16), out(BF16), out(BF16)],
        compiler_params=_params("arbitrary", "arbitrary"),
        name="odd_in_proj",
    )(x, g.reshape(1, d), shift, scale, w, clb)


def _t5_bucket(rel):
    half = NUM_BUCKETS // 2
    max_exact = half // 2
    n = jnp.abs(rel)
    nf = jnp.maximum(n, 1).astype(F32)
    large = max_exact + (jnp.log(nf / max_exact) / math.log(MAX_DISTANCE / max_exact)
                         * (half - max_exact)).astype(jnp.int32)
    large = jnp.minimum(large, half - 1)
    return jnp.where(rel > 0, half, 0) + jnp.where(n < max_exact, n, large)


def _bias_tiles(rel_bias):
    r = jnp.arange(LANES)[:, None]
    c = jnp.arange(LANES)[None, :]

    def lookup(rel, table):
        hot = jax.nn.one_hot(_t5_bucket(rel), NUM_BUCKETS, dtype=F32)
        return jnp.einsum('...k,kh->h...', hot, table.astype(F32), precision=lax.Precision.HIGHEST)

    rel_a = (jnp.arange(-2, 3) * LANES)[:, None, None] + (c - r)[None]
    rel_b = jnp.arange(3 * LANES)[None, :] - LANES - r
    return lookup(rel_a, rel_bias[:, :A_HEADS]), lookup(rel_b, rel_bias[:, A_HEADS:])


def _diff_attn_kernel(q_ref, k_ref, v_ref, bias_ref, qg_ref, kg_ref, lam_ref, sg_ref, o_ref, kn_scr, *, lam_init):
    i = pl.program_id(2)
    tq = q_ref.shape[1]
    s_len = k_ref.shape[1]
    first = lax.broadcasted_iota(jnp.int32, (1, LANES), 1) < HEAD_DIM

    @pl.when(i == 0)
    def _():
        kn_scr[...] = _norm_halves(k_ref[0].astype(F32), kg_ref[...], first).astype(BF16)

    bias = jnp.concatenate([jnp.concatenate([bias_ref[0, jnp.clip(j - (i * (tq // LANES) + a), -2, 2) + 2]
                                             for j in range(s_len // LANES)], axis=1)
                            for a in range(tq // LANES)], axis=0)

    lp = lam_ref[...]
    lam = (jnp.exp(jnp.sum(lp[0:1] * lp[1:2], axis=-1, keepdims=True))
           - jnp.exp(jnp.sum(lp[2:3] * lp[3:4], axis=-1, keepdims=True)) + lam_init)

    qn = _norm_halves(q_ref[0].astype(F32), qg_ref[...], first) * (HEAD_DIM ** -0.5 * LOG2E)
    kn = kn_scr[...]
    qms = [jnp.where(first, qn, 0.0).astype(BF16), jnp.where(first, 0.0, qn).astype(BF16)]
    scs = [lax.dot_general(qm, kn, _TRANS_B, preferred_element_type=F32) + bias for qm in qms]
    es = [jnp.exp2(sc - jnp.max(sc, axis=-1, keepdims=True)) for sc in scs]
    rs = [1.0 / jnp.sum(e, axis=-1, keepdims=True) for e in es]
    evs = [_dot(e.astype(BF16), v_ref[0]) for e in es]
    o = evs[0] * rs[0] - evs[1] * (lam * rs[1])
    o = o * lax.rsqrt(jnp.mean(o * o, axis=-1, keepdims=True) + EPS) * sg_ref[...] * (1.0 - lam_init)
    o_ref[0] = o.astype(BF16)


def _diff_attn_call(qkv, tiles_a, qk_g, lam_p, subln_g, lam_init):
    bsz, s, _ = qkv.shape
    tq = min(DIFF_ATTN_ROWS, s)
    dup = lambda g: jnp.concatenate([g, g]).reshape(1, LANES)
    return pl.pallas_call(
        functools.partial(_diff_attn_kernel, lam_init=lam_init),
        grid=(bsz, A_HEADS, s // tq),
        in_specs=[pl.BlockSpec((1, tq, LANES), lambda b, h, i: (b, i, h)),
                  pl.BlockSpec((1, s, LANES), lambda b, h, i: (b, 0, A_HEADS + h)),
                  pl.BlockSpec((1, s, LANES), lambda b, h, i: (b, 0, 2 * A_HEADS + h)),
                  pl.BlockSpec((1, 5, LANES, LANES), lambda b, h, i: (h, 0, 0, 0)),
                  pl.BlockSpec((1, LANES), lambda b, h, i: (0, 0)),
                  pl.BlockSpec((1, LANES), lambda b, h, i: (0, 0)),
                  pl.BlockSpec(lam_p.shape, lambda b, h, i: (0, 0)),
                  pl.BlockSpec((1, LANES), lambda b, h, i: (0, 0))],
        out_specs=pl.BlockSpec((1, tq, LANES), lambda b, h, i: (b, i, h)),
        out_shape=jax.ShapeDtypeStruct((bsz, s, A_HEADS * LANES), BF16),
        scratch_shapes=[pltpu.VMEM((s, LANES), BF16)],
        compiler_params=_params("arbitrary", "arbitrary", "arbitrary"),
        name="diff_attn",
    )(qkv, qkv, qkv, tiles_a * LOG2E, dup(qk_g[0]), dup(qk_g[1]), lam_p, subln_g.reshape(1, LANES))


def _win_attn_kernel(q_ref, k_ref, v_ref, bias_ref, qg_ref, kg_ref, sink_ref, o_ref, kn_scr, *, q_blocks):
    i = pl.program_id(1)
    nb = pl.num_programs(1) * q_blocks
    group = B_Q_HEADS // B_KV_HEADS
    first = lax.broadcasted_iota(jnp.int32, (1, LANES), 1) < HEAD_DIM

    @pl.when(i == 0)
    def _():
        kn_scr[...] = _norm_halves(k_ref[0].astype(F32), kg_ref[...], first).astype(BF16)

    r = lax.broadcasted_iota(jnp.int32, (group * LANES, 1), 0) & (LANES - 1)
    c = lax.broadcasted_iota(jnp.int32, (1, 3 * LANES), 1)
    band = jnp.abs(c - LANES - r) <= WINDOW
    sinks = [jnp.max(sink_ref[kv], axis=-1, keepdims=True) for kv in range(B_KV_HEADS)]

    for u in range(q_blocks):
        blk = i * q_blocks + u
        rows = slice(u * LANES, (u + 1) * LANES)
        starts = [pl.multiple_of(j * LANES, LANES)
                  for j in (jnp.maximum(blk - 1, 0), blk, jnp.minimum(blk + 1, nb - 1))]
        k3 = jnp.concatenate([kn_scr[pl.ds(st, LANES), :] for st in starts], axis=0)
        v3 = jnp.concatenate([v_ref[0, pl.ds(st, LANES), :] for st in starts], axis=0)
        valid = band & ((c >= LANES) | (blk > 0)) & ((c < 2 * LANES) | (blk < nb - 1))

        pairs = []
        for pb in range(B_Q_HEADS // 2):
            qp = _norm_halves(q_ref[0, rows, pb * LANES:(pb + 1) * LANES].astype(F32), qg_ref[...], first)
            qp = qp * (HEAD_DIM ** -0.5 * LOG2E)
            pairs.append((qp, pltpu.roll(qp, HEAD_DIM, 1)))

        for kv in range(B_KV_HEADS):
            kv_lanes = first if kv == 0 else jnp.logical_not(first)
            pbs = range(kv * group // 2, (kv + 1) * group // 2)
            qs = jnp.concatenate([jnp.where(kv_lanes, pairs[pb][0 if a == kv else 1], 0.0).astype(BF16)
                                  for pb in pbs for a in range(2)], axis=0)
            sc = lax.dot_general(qs, k3, _TRANS_B, preferred_element_type=F32) + bias_ref[kv]
            sc = jnp.where(valid, sc, -jnp.inf)
            sk = sinks[kv]
            m = jnp.maximum(jnp.max(sc, axis=-1, keepdims=True), sk)
            e = jnp.exp2(sc - m)
            p = e * (1.0 / (jnp.sum(e, axis=-1, keepdims=True) + jnp.exp2(sk - m)))
            o = _dot(p.astype(BF16), v3)
            for n, pb in enumerate(pbs):
                halves = []
                for a in range(2):
                    oh = o[(2 * n + a) * LANES:(2 * n + a + 1) * LANES]
                    halves.append(oh if a == kv else pltpu.roll(oh, HEAD_DIM, 1))
                o_ref[0, rows, pb * LANES:(pb + 1) * LANES] = jnp.where(first, halves[0], halves[1]).astype(BF16)


def _win_attn_call(qkv, tiles_b, qk_g, sink):
    bsz, s, n = qkv.shape
    qw = B_Q_HEADS * HEAD_DIM
    group = B_Q_HEADS // B_KV_HEADS
    q_blocks = min(4, s // LANES)
    tq = q_blocks * LANES
    q_blk = (2 * A_HEADS * LANES + A_HEADS * LANES) // qw
    k_blk = (2 * A_HEADS * LANES + A_HEADS * LANES + qw) // LANES
    dup = lambda g: jnp.concatenate([g, g]).reshape(1, LANES)
    bias = tiles_b.reshape(B_KV_HEADS, group * LANES, 3 * LANES) * LOG2E
    sink_rows = jnp.broadcast_to(jnp.repeat(sink.astype(F32).reshape(B_KV_HEADS, group) * LOG2E, LANES, axis=1)[..., None],
                                 (B_KV_HEADS, group * LANES, LANES))
    return pl.pallas_call(
        functools.partial(_win_attn_kernel, q_blocks=q_blocks),
        grid=(bsz, s // tq),
        in_specs=[pl.BlockSpec((1, tq, qw), lambda b, i: (b, i, q_blk)),
                  pl.BlockSpec((1, s, LANES), lambda b, i: (b, 0, k_blk)),
                  pl.BlockSpec((1, s, LANES), lambda b, i: (b, 0, k_blk + 1)),
                  pl.BlockSpec(bias.shape, lambda b, i: (0, 0, 0)),
                  pl.BlockSpec((1, LANES), lambda b, i: (0, 0)),
                  pl.BlockSpec((1, LANES), lambda b, i: (0, 0)),
                  pl.BlockSpec(sink_rows.shape, lambda b, i: (0, 0, 0))],
        out_specs=pl.BlockSpec((1, tq, qw), lambda b, i: (b, i, 0)),
        out_shape=jax.ShapeDtypeStruct((bsz, s, qw), BF16),
        scratch_shapes=[pltpu.VMEM((s, LANES), BF16)],
        compiler_params=_params("arbitrary", "arbitrary"),
        name="win_attn",
    )(qkv, qkv, qkv, bias, dup(qk_g[2]), dup(qk_g[3]), sink_rows)


def _gla_constants(cs):
    t = np.arange(cs)[:, None]
    u = np.arange(cs)[None, :]
    sums, pairs = [], []
    for rev in (False, True):
        mats = [(u >= t) if rev else (u <= t)]
        for h in (1, 2, 4):
            start = (t // (2 * h)) * 2 * h
            later = (t // h) % 2 == 1
            if rev:
                b = start + h
                mats.append(np.where(later, (u >= b) & (u < t), (u >= t) & (u < b)))
            else:
                b = start + h - 1
                mats.append(np.where(later, (u > b) & (u <= t), (u > t) & (u <= b)))
        sums.append(np.concatenate(mats, axis=0))
        lv, h = [], 1
        while h < cs:
            same = (t // (2 * h)) == (u // (2 * h))
            t_later = (t // h) % 2 == 1
            u_later = (u // h) % 2 == 1
            lv.append(same & (t_later != rev) & (u_later == rev))
            h *= 2
        pairs.append(np.stack(lv))
    return jnp.asarray(np.stack(sums), BF16), jnp.asarray(np.stack(pairs), F32)


def _gla_chunks(chains, pairs_ref):
    cs = chains[0][0].shape[0]
    ex_alls = []
    for q, k, v, lf2, st, sums, rev in chains:
        hi = lf2.astype(BF16)
        lo = (lf2 - hi.astype(F32)).astype(BF16)
        ex_alls.append(_dot(sums, hi) + _dot(sums, lo))

    outs, states, gcs = [], [], []
    for (q, k, v, lf2, st, sums, rev), ex_all in zip(chains, ex_alls):
        gc = ex_all[0:cs]
        tot = gc[0:1] if rev else gc[cs - 1:cs]
        o = lax.dot_general((q * jnp.exp2(gc)).astype(BF16), st.astype(BF16), _TRANS_B, preferred_element_type=F32)
        kd = (k * jnp.exp2(tot - gc)).astype(BF16)
        states.append(jnp.exp2(tot) * st + lax.dot_general(v, kd, _TRANS_A, preferred_element_type=F32))
        outs.append(o + jnp.sum(q * k, axis=-1, keepdims=True) * v.astype(F32))
        gcs.append(gc)

    attns = [jnp.zeros((cs, cs), F32) for _ in chains]
    for level in range(cs.bit_length() - 1):
        h = 1 << level
        for n, ((q, k, v, lf2, st, sums, rev), ex_all, gc) in enumerate(zip(chains, ex_alls, gcs)):
            if h < SUBLANES:
                ex = ex_all[(level + 1) * cs:(level + 2) * cs]
            else:
                bounds = [p * 2 * h + (h if rev else h - 1) for p in range(cs // (2 * h))]
                ref = jnp.concatenate([jnp.broadcast_to(gc[b:b + 1], (2 * h, gc.shape[1])) for b in bounds], axis=0)
                d = gc - ref
                ex = jnp.minimum(d, -d)
            x = jnp.exp2(ex)
            pr = lax.dot_general((q * x).astype(BF16), (k * x).astype(BF16), _TRANS_B, preferred_element_type=F32)
            attns[n] = attns[n] + pr * pairs_ref[int(rev), level]
    return [(o + _dot(attn.astype(BF16), ch[2]), st) for o, attn, ch, st in zip(outs, attns, chains, states)]


def _gla_kernel(qs_ref, lff_ref, lfb_ref, kf_ref, kb_ref, v_ref, gs_ref, og_ref, sums_ref, pairs_ref, y_ref,
                of_scr, ob_scr):
    s_len = qs_ref.shape[1]
    heads = qs_ref.shape[2] // LANES
    cs = sums_ref.shape[2]
    nc = s_len // cs
    specs = [(j, rev, lf_ref, k_ref, o_scr) for j in range(heads)
             for rev, lf_ref, k_ref, o_scr in ((False, lff_ref, kf_ref, of_scr), (True, lfb_ref, kb_ref, ob_scr))]

    def body(c, states):
        where = [(pl.ds(pl.multiple_of(((nc - 1 - c) if rev else c) * cs, cs), cs), slice(j * LANES, (j + 1) * LANES))
                 for j, rev, *_ in specs]
        chains = [(qs_ref[0, r, l].astype(F32), k_ref[0, r, l].astype(F32), v_ref[0, r, l], lf_ref[0, r, l],
                   st, sums_ref[int(rev)], rev)
                  for (r, l), st, (j, rev, lf_ref, k_ref, _) in zip(where, states, specs)]
        outs = _gla_chunks(chains, pairs_ref)
        for (r, l), (o, _), (*_, o_scr) in zip(where, outs, specs):
            o_scr[r, l] = o
        return tuple(st for _, st in outs)

    zero = jnp.zeros((LANES, LANES), F32)
    lax.fori_loop(0, nc, body, (zero,) * len(specs))
    for j in range(heads):
        l = slice(j * LANES, (j + 1) * LANES)
        o = of_scr[:, l] + ob_scr[:, l]
        y = o * lax.rsqrt(jnp.mean(o * o, axis=-1, keepdims=True) + EPS) * og_ref[...]
        y_ref[0, :, l] = (y * gs_ref[0, :, l].astype(F32)).astype(BF16)


def _gla_call(qs, lff, lfb, kf, kb, v, gs, out_g):
    bsz, s, hk = qs.shape
    sums, pairs = _gla_constants(min(GLA_CHUNK, s))
    width = GLA_HEADS_PER_STEP * LANES
    blk = pl.BlockSpec((1, s, width), lambda b, h: (b, 0, h))
    return pl.pallas_call(
        _gla_kernel,
        grid=(bsz, hk // width),
        in_specs=[blk] * 7 + [pl.BlockSpec((1, LANES), lambda b, h: (0, 0)),
                              pl.BlockSpec(sums.shape, lambda b, h: (0, 0, 0)),
                              pl.BlockSpec(pairs.shape, lambda b, h: (0, 0, 0, 0))],
        out_specs=blk,
        out_shape=jax.ShapeDtypeStruct((bsz, s, hk), BF16),
        scratch_shapes=[pltpu.VMEM((s, width), F32), pltpu.VMEM((s, width), F32)],
        compiler_params=_params("arbitrary", "arbitrary"),
        name="gla",
    )(qs, lff, lfb, kf, kb, v, gs, out_g.reshape(1, LANES), sums, pairs)


def kernel(x, c, ada_w, ada_b, norm_g, ffn_up, ffn_down, even_w_in, even_w_out, qk_norm_g, diff_lambda,
           diff_subln_g, sink_logit, rel_bias, odd_w_in, odd_w_out, c_lower_bound, c_out_norm_g):
    bsz, s, d = x.shape
    depth = ada_w.shape[0]
    assert d == 2 * A_HEADS * LANES == B_Q_HEADS * LANES == C_HEADS * LANES
    assert s % 256 == 0 or s == LANES

    mod_all = _mod_call(c, ada_w, ada_b)
    tiles_a, tiles_b = _bias_tiles(rel_bias)
    for l in range(depth):
        mod = mod_all[l].reshape(bsz, N_MOD, 1, d)
        sh1, sc1, g1, sh2, sc2, g2, sh3, sc3, g3 = (mod[:, k] for k in range(N_MOD))
        x = _ffn_call(x, norm_g[l, 0], sh1, sc1, g1, ffn_up[l, 0].astype(BF16), ffn_down[l, 0].astype(BF16))
        if l % 2 == 0:
            e = l // 2
            lam_init = 0.8 - 0.6 * math.exp(-0.3 * l)
            qkv = _even_in_call(x, norm_g[l, 1], sh2, sc2, even_w_in[e].astype(BF16))
            ya = _diff_attn_call(qkv, tiles_a, qk_norm_g[e], diff_lambda[e].astype(F32), diff_subln_g[e], lam_init)
            yb = _win_attn_call(qkv, tiles_b, qk_norm_g[e], sink_logit[e])
            mixer = (g2, [ya, yb], even_w_out[e].astype(BF16))
        else:
            o = l // 2
            parts = _odd_in_call(x, norm_g[l, 1], sh2, sc2, odd_w_in[o].astype(BF16),
                                 c_lower_bound.astype(F32), l)
            mixer = (g2, [_gla_call(*parts, c_out_norm_g[o])], odd_w_out[o].astype(BF16))
        x = _ffn_call(x, norm_g[l, 2], sh3, sc3, g3, ffn_up[l, 1].astype(BF16), ffn_down[l, 1].astype(BF16), mixer)
    return x
```

```python
import functools
import math

import jax
import jax.numpy as jnp
import numpy as np
from jax import lax
from jax.experimental import pallas as pl
from jax.experimental.pallas import tpu as pltpu

F32 = jnp.float32
BF16 = jnp.bfloat16

EPS = 1e-6
N_MOD = 9
HEAD_DIM = 64
LANES = 128
A_HEADS = 4
B_Q_HEADS = 8
B_KV_HEADS = 2
WINDOW = 128
NUM_BUCKETS = 32
MAX_DISTANCE = 128
C_HEADS = 8
FFN_CHUNK = 1024
FFN_ROWS = 1024
PROJ_ROWS = 512
MOD_COLS = 1152
DIFF_ATTN_ROWS = 512
WIN_ATTN_BLOCKS = 4
SUBLANES = 8
GLA_CHUNK = 128
GLA_CHUNKS_PER_STEP = 2
GLA_HEADS_PER_STEP = 2
LOG2E = math.log2(math.e)
V7X_VMEM_LIMIT_BYTES = 56 * 1024 * 1024

_TRANS_B = (((1,), (1,)), ((), ()))
_TRANS_A = (((0,), (0,)), ((), ()))


def _params(*sem):
    return pltpu.CompilerParams(dimension_semantics=sem, vmem_limit_bytes=V7X_VMEM_LIMIT_BYTES)


def _dot(a, b):
    return jnp.dot(a, b, preferred_element_type=F32)


def _silu(t):
    u = 0.5 * t
    return u + u * jnp.tanh(u)


def _norm_mod(x, g, shift, scale):
    ms = jnp.mean(x * x, axis=-1, keepdims=True)
    y = x * lax.rsqrt(ms + EPS) * g
    return y * (1.0 + scale) + shift


def _norm_halves(x, g, first):
    sq = x * x
    s0 = jnp.sum(jnp.where(first, sq, 0.0), axis=-1, keepdims=True)
    s1 = jnp.sum(jnp.where(first, 0.0, sq), axis=-1, keepdims=True)
    ms = jnp.where(first, s0, s1) * (1.0 / HEAD_DIM)
    return x * lax.rsqrt(ms + EPS) * g


def _mod_kernel(c_ref, w_ref, b_ref, o_ref):
    sc = _silu(c_ref[...])
    o_ref[0] = jnp.dot(sc, w_ref[0], precision=lax.Precision.HIGHEST,
                       preferred_element_type=F32) + b_ref[0]


def _mod_call(c, ada_w, ada_b):
    depth, d, m = ada_w.shape
    bsz = c.shape[0]
    tn = MOD_COLS
    assert m % tn == 0
    return pl.pallas_call(
        _mod_kernel,
        grid=(depth, m // tn),
        in_specs=[pl.BlockSpec((bsz, d), lambda l, j: (0, 0)),
                  pl.BlockSpec((1, d, tn), lambda l, j: (l, 0, j)),
                  pl.BlockSpec((1, 1, tn), lambda l, j: (l, 0, j))],
        out_specs=pl.BlockSpec((1, bsz, tn), lambda l, j: (l, 0, j)),
        out_shape=jax.ShapeDtypeStruct((depth, bsz, m), F32),
        compiler_params=_params("arbitrary", "arbitrary"),
        name="adaln_mod",
    )(c, ada_w, ada_b.reshape(depth, 1, m))


def _ffn_kernel(x_ref, g_ref, sh_ref, sc_ref, gate_ref, wu_ref, wd_ref, *rest, chunks):
    o_ref = rest[-1]
    x = x_ref[0]
    if len(rest) > 1:
        n_in = (len(rest) - 2) // 2
        mix = _dot(rest[1][0], rest[1 + n_in][...])
        for y_ref, w_ref in zip(rest[2:1 + n_in], rest[2 + n_in:1 + 2 * n_in]):
            mix += _dot(y_ref[0], w_ref[...])
        x = x + rest[0][0] * mix
    h = _norm_mod(x, g_ref[...], sh_ref[0], sc_ref[0]).astype(BF16)
    f = wd_ref.shape[0]
    acc = None
    for c0, cw in chunks:
        a = _dot(h, wu_ref[:, c0:c0 + cw])
        b = _dot(h, wu_ref[:, f + c0:f + c0 + cw])
        part = _dot((_silu(a) * b).astype(BF16), wd_ref[c0:c0 + cw, :])
        acc = part if acc is None else acc + part
    o_ref[0] = x + 0.5 * gate_ref[0] * acc


def _ffn_call(x, g, shift, scale, gate, w_up, w_down, mixer=None):
    bsz, s, d = x.shape
    f = w_down.shape[0]
    tm = min(FFN_ROWS, s)
    chunks = tuple((c0, min(FFN_CHUNK, f - c0)) for c0 in range(0, f, FFN_CHUNK))
    row = lambda b, i: (b, i, 0)
    vec = lambda b, i: (b, 0, 0)
    resident = dict(pipeline_mode=pl.Buffered(1))
    in_specs = [pl.BlockSpec((1, tm, d), row),
                pl.BlockSpec((1, d), lambda b, i: (0, 0)),
                pl.BlockSpec((1, 1, d), vec),
                pl.BlockSpec((1, 1, d), vec),
                pl.BlockSpec((1, 1, d), vec),
                pl.BlockSpec(w_up.shape, lambda b, i: (0, 0), **resident),
                pl.BlockSpec(w_down.shape, lambda b, i: (0, 0), **resident)]
    args = [x, g.reshape(1, d), shift, scale, gate, w_up, w_down]
    if mixer is not None:
        mixer_gate, ys, w_out = mixer
        wk = ys[0].shape[-1]
        in_specs += ([pl.BlockSpec((1, 1, d), vec)] + [pl.BlockSpec((1, tm, wk), row)] * len(ys)
                     + [pl.BlockSpec((wk, d), functools.partial(lambda b, i, k: (k, 0), k=k), **resident)
                        for k in range(len(ys))])
        args += [mixer_gate, *ys, *([w_out] * len(ys))]
    return pl.pallas_call(
        functools.partial(_ffn_kernel, chunks=chunks),
        grid=(bsz, s // tm),
        in_specs=in_specs,
        out_specs=pl.BlockSpec((1, tm, d), row),
        out_shape=jax.ShapeDtypeStruct(x.shape, F32),
        compiler_params=_params("arbitrary", "arbitrary"),
        name="ffn",
    )(*args)


def _even_in_kernel(x_ref, g_ref, sh_ref, sc_ref, w_ref, o_ref):
    h = _norm_mod(x_ref[0], g_ref[...], sh_ref[0], sc_ref[0]).astype(BF16)
    o_ref[0] = _dot(h, w_ref[...]).astype(BF16)


def _even_in_call(x, g, shift, scale, w):
    bsz, s, d = x.shape
    n = w.shape[1]
    tm = min(PROJ_ROWS, s)
    row = lambda b, i: (b, i, 0)
    vec = lambda b, i: (b, 0, 0)
    return pl.pallas_call(
        _even_in_kernel,
        grid=(bsz, s // tm),
        in_specs=[pl.BlockSpec((1, tm, d), row),
                  pl.BlockSpec((1, d), lambda b, i: (0, 0)),
                  pl.BlockSpec((1, 1, d), vec),
                  pl.BlockSpec((1, 1, d), vec),
                  pl.BlockSpec((d, n), lambda b, i: (0, 0))],
        out_specs=pl.BlockSpec((1, tm, n), row),
        out_shape=jax.ShapeDtypeStruct((bsz, s, n), BF16),
        compiler_params=_params("arbitrary", "arbitrary"),
        name="even_in_proj",
    )(x, g.reshape(1, d), shift, scale, w)


def _log_forget_and_input_gate(f_raw, clb, layer):
    e = jnp.exp(clb - jnp.max(clb, axis=0, keepdims=True))
    lb = jnp.sum(e[1:layer + 1], axis=0, keepdims=True) / jnp.sum(e, axis=0, keepdims=True)
    half_th = 0.5 * jnp.tanh(0.5 * f_raw)
    f = lb + (1.0 - lb) * (0.5 + half_th)
    lf2 = jnp.where(f > 0.0, jnp.log2(f), (jnp.log(1.0 - lb) + f_raw) * LOG2E)
    return lf2, (1.0 - lb) * (0.5 - half_th)


def _odd_in_kernel(x_ref, g_ref, sh_ref, sc_ref, w_ref, clb_ref,
                   qs_ref, lff_ref, lfb_ref, kf_ref, kb_ref, v_ref, gs_ref, *, layer):
    h = _norm_mod(x_ref[0], g_ref[...], sh_ref[0], sc_ref[0]).astype(BF16)
    hk = qs_ref.shape[-1]
    seg = lambda n: _dot(h, w_ref[:, n * hk:(n + 1) * hk])
    qs_ref[0] = _silu(seg(0)).astype(BF16)
    for n, lf_ref, k_ref in ((1, lff_ref, kf_ref), (2, lfb_ref, kb_ref)):
        lf2, k = _log_forget_and_input_gate(seg(n), clb_ref[n - 1], layer)
        lf_ref[0] = lf2
        k_ref[0] = k.astype(BF16)
    v_ref[0] = seg(3).astype(BF16)
    gs_ref[0] = _silu(seg(4)).astype(BF16)


def _odd_in_call(x, g, shift, scale, w, clb, layer):
    bsz, s, d = x.shape
    hk = clb.shape[-1]
    tm = min(PROJ_ROWS, s)
    row = lambda b, i: (b, i, 0)
    vec = lambda b, i: (b, 0, 0)
    out = lambda dt: jax.ShapeDtypeStruct((bsz, s, hk), dt)
    return pl.pallas_call(
        functools.partial(_odd_in_kernel, layer=layer),
        grid=(bsz, s // tm),
        in_specs=[pl.BlockSpec((1, tm, d), row),
                  pl.BlockSpec((1, d), lambda b, i: (0, 0)),
                  pl.BlockSpec((1, 1, d), vec),
                  pl.BlockSpec((1, 1, d), vec),
                  pl.BlockSpec(w.shape, lambda b, i: (0, 0), pipeline_mode=pl.Buffered(1)),
                  pl.BlockSpec(clb.shape, lambda b, i: (0, 0, 0))],
        out_specs=[pl.BlockSpec((1, tm, hk), row)] * 7,
        out_shape=[out(BF16), out(F32), out(F32), out(BF16), out(BF16), out(BF16), out(BF16)],
        compiler_params=_params("arbitrary", "arbitrary"),
        name="odd_in_proj",
    )(x, g.reshape(1, d), shift, scale, w, clb)


def _t5_bucket(rel):
    half = NUM_BUCKETS // 2
    max_exact = half // 2
    n = jnp.abs(rel)
    nf = jnp.maximum(n, 1).astype(F32)
    large = max_exact + (jnp.log(nf / max_exact) / math.log(MAX_DISTANCE / max_exact)
                         * (half - max_exact)).astype(jnp.int32)
    large = jnp.minimum(large, half - 1)
    return jnp.where(rel > 0, half, 0) + jnp.where(n < max_exact, n, large)


def _bias_tiles(rel_bias):
    r = jnp.arange(LANES)[:, None]
    c = jnp.arange(LANES)[None, :]

    def lookup(rel, table):
        hot = jax.nn.one_hot(_t5_bucket(rel), NUM_BUCKETS, dtype=F32)
        return jnp.einsum('...k,kh->h...', hot, table.astype(F32), precision=lax.Precision.HIGHEST)

    rel_a = (jnp.arange(-2, 3) * LANES)[:, None, None] + (c - r)[None]
    rel_b = jnp.arange(3 * LANES)[None, :] - LANES - r
    return lookup(rel_a, rel_bias[:, :A_HEADS]), lookup(rel_b, rel_bias[:, A_HEADS:])


def _diff_attn_kernel(q_ref, k_ref, v_ref, bias_ref, qg_ref, kg_ref, lam_ref, sg_ref, o_ref, kn_scr, *, lam_init):
    i = pl.program_id(2)
    tq = q_ref.shape[1]
    s_len = k_ref.shape[1]
    first = lax.broadcasted_iota(jnp.int32, (1, LANES), 1) < HEAD_DIM

    @pl.when(i == 0)
    def _():
        kn_scr[...] = _norm_halves(k_ref[0].astype(F32), kg_ref[...], first).astype(BF16)

    bias = jnp.concatenate([jnp.concatenate([bias_ref[0, jnp.clip(j - (i * (tq // LANES) + a), -2, 2) + 2]
                                             for j in range(s_len // LANES)], axis=1)
                            for a in range(tq // LANES)], axis=0)

    lp = lam_ref[...]
    lam = (jnp.exp(jnp.sum(lp[0:1] * lp[1:2], axis=-1, keepdims=True))
           - jnp.exp(jnp.sum(lp[2:3] * lp[3:4], axis=-1, keepdims=True)) + lam_init)

    qn = _norm_halves(q_ref[0].astype(F32), qg_ref[...], first) * (HEAD_DIM ** -0.5 * LOG2E)
    kn = kn_scr[...]
    qms = [jnp.where(first, qn, 0.0).astype(BF16), jnp.where(first, 0.0, qn).astype(BF16)]
    scs = [lax.dot_general(qm, kn, _TRANS_B, preferred_element_type=F32) + bias for qm in qms]
    es = [jnp.exp2(sc - jnp.max(sc, axis=-1, keepdims=True)) for sc in scs]
    rs = [1.0 / jnp.sum(e, axis=-1, keepdims=True) for e in es]
    evs = [_dot(e.astype(BF16), v_ref[0]) for e in es]
    o = evs[0] * rs[0] - evs[1] * (lam * rs[1])
    o = o * lax.rsqrt(jnp.mean(o * o, axis=-1, keepdims=True) + EPS) * sg_ref[...] * (1.0 - lam_init)
    o_ref[0] = o.astype(BF16)


def _diff_attn_call(qkv, tiles_a, qk_g, lam_p, subln_g, lam_init):
    bsz, s, _ = qkv.shape
    tq = min(DIFF_ATTN_ROWS, s)
    dup = lambda g: jnp.concatenate([g, g]).reshape(1, LANES)
    return pl.pallas_call(
        functools.partial(_diff_attn_kernel, lam_init=lam_init),
        grid=(bsz, A_HEADS, s // tq),
        in_specs=[pl.BlockSpec((1, tq, LANES), lambda b, h, i: (b, i, h)),
                  pl.BlockSpec((1, s, LANES), lambda b, h, i: (b, 0, A_HEADS + h)),
                  pl.BlockSpec((1, s, LANES), lambda b, h, i: (b, 0, 2 * A_HEADS + h)),
                  pl.BlockSpec((1, 5, LANES, LANES), lambda b, h, i: (h, 0, 0, 0)),
                  pl.BlockSpec((1, LANES), lambda b, h, i: (0, 0)),
                  pl.BlockSpec((1, LANES), lambda b, h, i: (0, 0)),
                  pl.BlockSpec(lam_p.shape, lambda b, h, i: (0, 0)),
                  pl.BlockSpec((1, LANES), lambda b, h, i: (0, 0))],
        out_specs=pl.BlockSpec((1, tq, LANES), lambda b, h, i: (b, i, h)),
        out_shape=jax.ShapeDtypeStruct((bsz, s, A_HEADS * LANES), BF16),
        scratch_shapes=[pltpu.VMEM((s, LANES), BF16)],
        compiler_params=_params("arbitrary", "arbitrary", "arbitrary"),
        name="diff_attn",
    )(qkv, qkv, qkv, tiles_a * LOG2E, dup(qk_g[0]), dup(qk_g[1]), lam_p, subln_g.reshape(1, LANES))


def _win_attn_kernel(q_ref, k_ref, v_ref, bias_ref, qg_ref, kg_ref, sink_ref, o_ref, kn_scr, *, q_blocks):
    i = pl.program_id(1)
    nb = pl.num_programs(1) * q_blocks
    group = B_Q_HEADS // B_KV_HEADS
    first = lax.broadcasted_iota(jnp.int32, (1, LANES), 1) < HEAD_DIM

    @pl.when(i == 0)
    def _():
        kn_scr[...] = _norm_halves(k_ref[0].astype(F32), kg_ref[...], first).astype(BF16)

    r = lax.broadcasted_iota(jnp.int32, (group * LANES, 1), 0) & (LANES - 1)
    c = lax.broadcasted_iota(jnp.int32, (1, 3 * LANES), 1)
    band = jnp.abs(c - LANES - r) <= WINDOW
    sinks = [jnp.max(sink_ref[kv], axis=-1, keepdims=True) for kv in range(B_KV_HEADS)]

    for u in range(q_blocks):
        blk = i * q_blocks + u
        rows = slice(u * LANES, (u + 1) * LANES)
        starts = [pl.multiple_of(j * LANES, LANES)
                  for j in (jnp.maximum(blk - 1, 0), blk, jnp.minimum(blk + 1, nb - 1))]
        k3 = jnp.concatenate([kn_scr[pl.ds(st, LANES), :] for st in starts], axis=0)
        v3 = jnp.concatenate([v_ref[0, pl.ds(st, LANES), :] for st in starts], axis=0)
        valid = band & ((c >= LANES) | (blk > 0)) & ((c < 2 * LANES) | (blk < nb - 1))

        pairs = []
        for pb in range(B_Q_HEADS // 2):
            qp = _norm_halves(q_ref[0, rows, pb * LANES:(pb + 1) * LANES].astype(F32), qg_ref[...], first)
            qp = qp * (HEAD_DIM ** -0.5 * LOG2E)
            pairs.append((qp, pltpu.roll(qp, HEAD_DIM, 1)))

        for kv in range(B_KV_HEADS):
            kv_lanes = first if kv == 0 else jnp.logical_not(first)
            pbs = range(kv * group // 2, (kv + 1) * group // 2)
            qs = jnp.concatenate([jnp.where(kv_lanes, pairs[pb][0 if a == kv else 1], 0.0).astype(BF16)
                                  for pb in pbs for a in range(2)], axis=0)
            sc = lax.dot_general(qs, k3, _TRANS_B, preferred_element_type=F32) + bias_ref[kv]
            sc = jnp.where(valid, sc, -jnp.inf)
            sk = sinks[kv]
            m = jnp.maximum(jnp.max(sc, axis=-1, keepdims=True), sk)
            e = jnp.exp2(sc - m)
            p = e * (1.0 / (jnp.sum(e, axis=-1, keepdims=True) + jnp.exp2(sk - m)))
            o = _dot(p.astype(BF16), v3)
            for n, pb in enumerate(pbs):
                halves = []
                for a in range(2):
                    oh = o[(2 * n + a) * LANES:(2 * n + a + 1) * LANES]
                    halves.append(oh if a == kv else pltpu.roll(oh, HEAD_DIM, 1))
                o_ref[0, rows, pb * LANES:(pb + 1) * LANES] = jnp.where(first, halves[0], halves[1]).astype(BF16)


def _win_attn_call(qkv, tiles_b, qk_g, sink):
    bsz, s, n = qkv.shape
    qw = B_Q_HEADS * HEAD_DIM
    group = B_Q_HEADS // B_KV_HEADS
    q_blocks = min(WIN_ATTN_BLOCKS, s // LANES)
    tq = q_blocks * LANES
    q_blk = (2 * A_HEADS * LANES + A_HEADS * LANES) // qw
    k_blk = (2 * A_HEADS * LANES + A_HEADS * LANES + qw) // LANES
    dup = lambda g: jnp.concatenate([g, g]).reshape(1, LANES)
    bias = tiles_b.reshape(B_KV_HEADS, group * LANES, 3 * LANES) * LOG2E
    sink_rows = jnp.broadcast_to(jnp.repeat(sink.astype(F32).reshape(B_KV_HEADS, group) * LOG2E, LANES, axis=1)[..., None],
                                 (B_KV_HEADS, group * LANES, LANES))
    return pl.pallas_call(
        functools.partial(_win_attn_kernel, q_blocks=q_blocks),
        grid=(bsz, s // tq),
        in_specs=[pl.BlockSpec((1, tq, qw), lambda b, i: (b, i, q_blk)),
                  pl.BlockSpec((1, s, LANES), lambda b, i: (b, 0, k_blk)),
                  pl.BlockSpec((1, s, LANES), lambda b, i: (b, 0, k_blk + 1)),
                  pl.BlockSpec(bias.shape, lambda b, i: (0, 0, 0)),
                  pl.BlockSpec((1, LANES), lambda b, i: (0, 0)),
                  pl.BlockSpec((1, LANES), lambda b, i: (0, 0)),
                  pl.BlockSpec(sink_rows.shape, lambda b, i: (0, 0, 0))],
        out_specs=pl.BlockSpec((1, tq, qw), lambda b, i: (b, i, 0)),
        out_shape=jax.ShapeDtypeStruct((bsz, s, qw), BF16),
        scratch_shapes=[pltpu.VMEM((s, LANES), BF16)],
        compiler_params=_params("arbitrary", "arbitrary"),
        name="win_attn",
    )(qkv, qkv, qkv, bias, dup(qk_g[2]), dup(qk_g[3]), sink_rows)


def _gla_constants(cs):
    t = np.arange(cs)[:, None]
    u = np.arange(cs)[None, :]
    sums, pairs = [], []
    for rev in (False, True):
        mats = [(u >= t) if rev else (u <= t)]
        for h in (2, 4):
            start = (t // (2 * h)) * 2 * h
            later = (t // h) % 2 == 1
            if rev:
                b = start + h
                mats.append(np.where(later, (u >= b) & (u < t), (u >= t) & (u < b)))
            else:
                b = start + h - 1
                mats.append(np.where(later, (u > b) & (u <= t), (u > t) & (u <= b)))
        sums.append(np.concatenate(mats, axis=0))
        lv, h = [], 1
        while h < cs:
            same = (t // (2 * h)) == (u // (2 * h))
            t_later = (t // h) % 2 == 1
            u_later = (u // h) % 2 == 1
            lv.append(same & (t_later != rev) & (u_later == rev))
            h *= 2
        pairs.append(np.stack(lv))
    return jnp.asarray(np.stack(sums), BF16), jnp.asarray(np.stack(pairs), F32)


def _gla_chunks(chains, pairs_ref):
    cs = chains[0][0].shape[0]
    odd_row = (lax.broadcasted_iota(jnp.int32, (cs, 1), 0) & 1) == 1
    ex_alls = []
    for q, k, v, lf2, st, sums, rev in chains:
        hi = lf2.astype(BF16)
        lo = (lf2 - hi.astype(F32)).astype(BF16)
        ex_alls.append(_dot(sums, hi) + _dot(sums, lo))

    outs, states, gcs = [], [], []
    for (q, k, v, lf2, st, sums, rev), ex_all in zip(chains, ex_alls):
        if isinstance(st, int):
            st = states[st]
        gc = ex_all[0:cs]
        tot = gc[0:1] if rev else gc[cs - 1:cs]
        o = lax.dot_general((q * jnp.exp2(gc)).astype(BF16), st.astype(BF16), _TRANS_B, preferred_element_type=F32)
        kd = (k * jnp.exp2(tot - gc)).astype(BF16)
        states.append(jnp.exp2(tot) * st + lax.dot_general(v, kd, _TRANS_A, preferred_element_type=F32))
        outs.append(o + jnp.sum(q * k, axis=-1, keepdims=True) * v.astype(F32))
        gcs.append(gc)

    attns = [jnp.zeros((cs, cs), F32) for _ in chains]
    for level in range(cs.bit_length() - 1):
        h = 1 << level
        for n, ((q, k, v, lf2, st, sums, rev), ex_all, gc) in enumerate(zip(chains, ex_alls, gcs)):
            if h == 1:
                ex = jnp.where(odd_row != rev, lf2, 0.0)
            elif h < SUBLANES:
                ex = ex_all[level * cs:(level + 1) * cs]
            else:
                bounds = [p * 2 * h + (h if rev else h - 1) for p in range(cs // (2 * h))]
                ref = jnp.concatenate([jnp.broadcast_to(gc[b:b + 1], (2 * h, gc.shape[1])) for b in bounds], axis=0)
                d = gc - ref
                ex = jnp.minimum(d, -d)
            x = jnp.exp2(ex)
            pr = lax.dot_general((q * x).astype(BF16), (k * x).astype(BF16), _TRANS_B, preferred_element_type=F32)
            attns[n] = attns[n] + pr * pairs_ref[int(rev), level]
    return [(o + _dot(attn.astype(BF16), ch[2]), st) for o, attn, ch, st in zip(outs, attns, chains, states)]


def _gla_kernel(qs_ref, lff_ref, lfb_ref, kf_ref, kb_ref, v_ref, gs_ref, og_ref, sums_ref, pairs_ref, y_ref,
                of_scr, ob_scr):
    s_len = qs_ref.shape[1]
    heads = qs_ref.shape[2] // LANES
    cs = sums_ref.shape[2]
    nc = s_len // cs
    specs = [(j, rev, lf_ref, k_ref, o_scr) for j in range(heads)
             for rev, lf_ref, k_ref, o_scr in ((False, lff_ref, kf_ref, of_scr), (True, lfb_ref, kb_ref, ob_scr))]

    n = len(specs)

    def body(c, states):
        where, chains = [], []
        for step in range(GLA_CHUNKS_PER_STEP):
            for m, (j, rev, lf_ref, k_ref, _) in enumerate(specs):
                chunk = c * GLA_CHUNKS_PER_STEP + step
                r = pl.ds(pl.multiple_of(((nc - 1 - chunk) if rev else chunk) * cs, cs), cs)
                l = slice(j * LANES, (j + 1) * LANES)
                where.append((r, l))
                chains.append((qs_ref[0, r, l].astype(F32), k_ref[0, r, l].astype(F32), v_ref[0, r, l],
                               lf_ref[0, r, l], states[m] if step == 0 else (step - 1) * n + m,
                               sums_ref[int(rev)], rev))
        outs = _gla_chunks(chains, pairs_ref)
        for (r, l), (o, _), (*_, o_scr) in zip(where, outs, specs * GLA_CHUNKS_PER_STEP):
            o_scr[r, l] = o
        return tuple(st for _, st in outs[-n:])

    zero = jnp.zeros((LANES, LANES), F32)
    lax.fori_loop(0, nc // GLA_CHUNKS_PER_STEP, body, (zero,) * n)
    for j in range(heads):
        l = slice(j * LANES, (j + 1) * LANES)
        o = of_scr[:, l] + ob_scr[:, l]
        y = o * lax.rsqrt(jnp.mean(o * o, axis=-1, keepdims=True) + EPS) * og_ref[...]
        y_ref[0, :, l] = (y * gs_ref[0, :, l].astype(F32)).astype(BF16)


def _gla_call(qs, lff, lfb, kf, kb, v, gs, out_g):
    bsz, s, hk = qs.shape
    sums, pairs = _gla_constants(min(GLA_CHUNK, s))
    width = GLA_HEADS_PER_STEP * LANES
    blk = pl.BlockSpec((1, s, width), lambda b, h: (b, 0, h))
    return pl.pallas_call(
        _gla_kernel,
        grid=(bsz, hk // width),
        in_specs=[blk] * 7 + [pl.BlockSpec((1, LANES), lambda b, h: (0, 0)),
                              pl.BlockSpec(sums.shape, lambda b, h: (0, 0, 0)),
                              pl.BlockSpec(pairs.shape, lambda b, h: (0, 0, 0, 0))],
        out_specs=blk,
        out_shape=jax.ShapeDtypeStruct((bsz, s, hk), BF16),
        scratch_shapes=[pltpu.VMEM((s, width), F32), pltpu.VMEM((s, width), F32)],
        compiler_params=_params("arbitrary", "arbitrary"),
        name="gla",
    )(qs, lff, lfb, kf, kb, v, gs, out_g.reshape(1, LANES), sums, pairs)


def kernel(x, c, ada_w, ada_b, norm_g, ffn_up, ffn_down, even_w_in, even_w_out, qk_norm_g, diff_lambda,
           diff_subln_g, sink_logit, rel_bias, odd_w_in, odd_w_out, c_lower_bound, c_out_norm_g):
    bsz, s, d = x.shape
    depth = ada_w.shape[0]
    assert d == 2 * A_HEADS * LANES == B_Q_HEADS * LANES == C_HEADS * LANES
    assert s % (GLA_CHUNK * GLA_CHUNKS_PER_STEP) == 0 and s % min(DIFF_ATTN_ROWS, s) == 0

    mod_all = _mod_call(c, ada_w, ada_b)
    tiles_a, tiles_b = _bias_tiles(rel_bias)
    for l in range(depth):
        mod = mod_all[l].reshape(bsz, N_MOD, 1, d)
        sh1, sc1, g1, sh2, sc2, g2, sh3, sc3, g3 = (mod[:, k] for k in range(N_MOD))
        x = _ffn_call(x, norm_g[l, 0], sh1, sc1, g1, ffn_up[l, 0].astype(BF16), ffn_down[l, 0].astype(BF16))
        if l % 2 == 0:
            e = l // 2
            lam_init = 0.8 - 0.6 * math.exp(-0.3 * l)
            qkv = _even_in_call(x, norm_g[l, 1], sh2, sc2, even_w_in[e].astype(BF16))
            ya = _diff_attn_call(qkv, tiles_a, qk_norm_g[e], diff_lambda[e].astype(F32), diff_subln_g[e], lam_init)
            yb = _win_attn_call(qkv, tiles_b, qk_norm_g[e], sink_logit[e])
            mixer = (g2, [ya, yb], even_w_out[e].astype(BF16))
        else:
            o = l // 2
            parts = _odd_in_call(x, norm_g[l, 1], sh2, sc2, odd_w_in[o].astype(BF16),
                                 c_lower_bound.astype(F32), l)
            mixer = (g2, [_gla_call(*parts, c_out_norm_g[o])], odd_w_out[o].astype(BF16))
        x = _ffn_call(x, norm_g[l, 2], sh3, sc3, g3, ffn_up[l, 1].astype(BF16), ffn_down[l, 1].astype(BF16), mixer)
    return x
```

```python
import functools
import math

import jax
import jax.numpy as jnp
import numpy as np
from jax import lax
from jax.experimental import pallas as pl
from jax.experimental.pallas import tpu as pltpu

F32 = jnp.float32
BF16 = jnp.bfloat16

EPS = 1e-6
N_MOD = 9
HEAD_DIM = 64
LANES = 128
A_HEADS = 4
B_Q_HEADS = 8
B_KV_HEADS = 2
WINDOW = 128
NUM_BUCKETS = 32
MAX_DISTANCE = 128
C_HEADS = 8
FFN_CHUNK = 1024
FFN_ROWS = 1024
PROJ_ROWS = 512
MOD_COLS = 1152
DIFF_ATTN_ROWS = 512
WIN_ATTN_BLOCKS = 4
SUBLANES = 8
GLA_CHUNK = 128
GLA_CHUNKS_PER_STEP = 2
GLA_HEADS_PER_STEP = 2
LOG2E = math.log2(math.e)
V7X_VMEM_LIMIT_BYTES = 56 * 1024 * 1024

_TRANS_B = (((1,), (1,)), ((), ()))
_TRANS_A = (((0,), (0,)), ((), ()))


def _params(*sem):
    return pltpu.CompilerParams(dimension_semantics=sem, vmem_limit_bytes=V7X_VMEM_LIMIT_BYTES)


def _dot(a, b):
    return jnp.dot(a, b, preferred_element_type=F32)


def _silu(t):
    u = 0.5 * t
    return u + u * jnp.tanh(u)


def _norm_mod(x, g, shift, scale):
    ms = jnp.mean(x * x, axis=-1, keepdims=True)
    y = x * lax.rsqrt(ms + EPS) * g
    return y * (1.0 + scale) + shift


def _norm_halves(x, g, first):
    sq = x * x
    s0 = jnp.sum(jnp.where(first, sq, 0.0), axis=-1, keepdims=True)
    s1 = jnp.sum(jnp.where(first, 0.0, sq), axis=-1, keepdims=True)
    ms = jnp.where(first, s0, s1) * (1.0 / HEAD_DIM)
    return x * lax.rsqrt(ms + EPS) * g


def _mod_kernel(c_ref, w_ref, b_ref, o_ref):
    sc = _silu(c_ref[...])
    o_ref[0] = jnp.dot(sc, w_ref[0], precision=lax.Precision.HIGHEST,
                       preferred_element_type=F32) + b_ref[0]


def _mod_call(c, ada_w, ada_b):
    depth, d, m = ada_w.shape
    bsz = c.shape[0]
    tn = MOD_COLS
    assert m % tn == 0
    return pl.pallas_call(
        _mod_kernel,
        grid=(depth, m // tn),
        in_specs=[pl.BlockSpec((bsz, d), lambda l, j: (0, 0)),
                  pl.BlockSpec((1, d, tn), lambda l, j: (l, 0, j)),
                  pl.BlockSpec((1, 1, tn), lambda l, j: (l, 0, j))],
        out_specs=pl.BlockSpec((1, bsz, tn), lambda l, j: (l, 0, j)),
        out_shape=jax.ShapeDtypeStruct((depth, bsz, m), F32),
        compiler_params=_params("arbitrary", "arbitrary"),
        name="adaln_mod",
    )(c, ada_w, ada_b.reshape(depth, 1, m))


def _ffn_kernel(x_ref, g_ref, sh_ref, sc_ref, gate_ref, wu_ref, wd_ref, *rest, chunks):
    o_ref = rest[-1]
    x = x_ref[0]
    if len(rest) > 1:
        n_in = (len(rest) - 2) // 2
        mix = _dot(rest[1][0], rest[1 + n_in][...])
        for y_ref, w_ref in zip(rest[2:1 + n_in], rest[2 + n_in:1 + 2 * n_in]):
            mix += _dot(y_ref[0], w_ref[...])
        x = x + rest[0][0] * mix
    h = _norm_mod(x, g_ref[...], sh_ref[0], sc_ref[0]).astype(BF16)
    f = wd_ref.shape[0]
    acc = None
    for c0, cw in chunks:
        a = _dot(h, wu_ref[:, c0:c0 + cw])
        b = _dot(h, wu_ref[:, f + c0:f + c0 + cw])
        part = _dot((_silu(a) * b).astype(BF16), wd_ref[c0:c0 + cw, :])
        acc = part if acc is None else acc + part
    o_ref[0] = x + 0.5 * gate_ref[0] * acc


def _ffn_call(x, g, shift, scale, gate, w_up, w_down, mixer=None):
    bsz, s, d = x.shape
    f = w_down.shape[0]
    tm = min(FFN_ROWS, s)
    chunks = tuple((c0, min(FFN_CHUNK, f - c0)) for c0 in range(0, f, FFN_CHUNK))
    row = lambda b, i: (b, i, 0)
    vec = lambda b, i: (b, 0, 0)
    resident = dict(pipeline_mode=pl.Buffered(1))
    in_specs = [pl.BlockSpec((1, tm, d), row),
                pl.BlockSpec((1, d), lambda b, i: (0, 0)),
                pl.BlockSpec((1, 1, d), vec),
                pl.BlockSpec((1, 1, d), vec),
                pl.BlockSpec((1, 1, d), vec),
                pl.BlockSpec(w_up.shape, lambda b, i: (0, 0), **resident),
                pl.BlockSpec(w_down.shape, lambda b, i: (0, 0), **resident)]
    args = [x, g.reshape(1, d), shift, scale, gate, w_up, w_down]
    if mixer is not None:
        mixer_gate, ys, w_out = mixer
        wk = ys[0].shape[-1]
        in_specs += ([pl.BlockSpec((1, 1, d), vec)] + [pl.BlockSpec((1, tm, wk), row)] * len(ys)
                     + [pl.BlockSpec((wk, d), functools.partial(lambda b, i, k: (k, 0), k=k), **resident)
                        for k in range(len(ys))])
        args += [mixer_gate, *ys, *([w_out] * len(ys))]
    return pl.pallas_call(
        functools.partial(_ffn_kernel, chunks=chunks),
        grid=(bsz, s // tm),
        in_specs=in_specs,
        out_specs=pl.BlockSpec((1, tm, d), row),
        out_shape=jax.ShapeDtypeStruct(x.shape, F32),
        compiler_params=_params("arbitrary", "arbitrary"),
        name="ffn",
    )(*args)


def _even_in_kernel(x_ref, g_ref, sh_ref, sc_ref, w_ref, o_ref):
    h = _norm_mod(x_ref[0], g_ref[...], sh_ref[0], sc_ref[0]).astype(BF16)
    o_ref[0] = _dot(h, w_ref[...]).astype(BF16)


def _even_in_call(x, g, shift, scale, w):
    bsz, s, d = x.shape
    n = w.shape[1]
    tm = min(PROJ_ROWS, s)
    row = lambda b, i: (b, i, 0)
    vec = lambda b, i: (b, 0, 0)
    return pl.pallas_call(
        _even_in_kernel,
        grid=(bsz, s // tm),
        in_specs=[pl.BlockSpec((1, tm, d), row),
                  pl.BlockSpec((1, d), lambda b, i: (0, 0)),
                  pl.BlockSpec((1, 1, d), vec),
                  pl.BlockSpec((1, 1, d), vec),
                  pl.BlockSpec((d, n), lambda b, i: (0, 0))],
        out_specs=pl.BlockSpec((1, tm, n), row),
        out_shape=jax.ShapeDtypeStruct((bsz, s, n), BF16),
        compiler_params=_params("arbitrary", "arbitrary"),
        name="even_in_proj",
    )(x, g.reshape(1, d), shift, scale, w)


def _log_forget_and_input_gate(f_raw, clb, layer):
    e = jnp.exp(clb - jnp.max(clb, axis=0, keepdims=True))
    lb = jnp.sum(e[1:layer + 1], axis=0, keepdims=True) / jnp.sum(e, axis=0, keepdims=True)
    half_th = 0.5 * jnp.tanh(0.5 * f_raw)
    f = lb + (1.0 - lb) * (0.5 + half_th)
    lf2 = jnp.where(f > 0.0, jnp.log2(f), (jnp.log(1.0 - lb) + f_raw) * LOG2E)
    return lf2, (1.0 - lb) * (0.5 - half_th)


def _odd_in_kernel(x_ref, g_ref, sh_ref, sc_ref, w_ref, clb_ref,
                   qs_ref, lff_ref, lfb_ref, kf_ref, kb_ref, v_ref, gs_ref, *, layer):
    h = _norm_mod(x_ref[0], g_ref[...], sh_ref[0], sc_ref[0]).astype(BF16)
    hk = qs_ref.shape[-1]
    seg = lambda n: _dot(h, w_ref[:, n * hk:(n + 1) * hk])
    qs_ref[0] = _silu(seg(0)).astype(BF16)
    for n, lf_ref, k_ref in ((1, lff_ref, kf_ref), (2, lfb_ref, kb_ref)):
        lf2, k = _log_forget_and_input_gate(seg(n), clb_ref[n - 1], layer)
        lf_ref[0] = lf2
        k_ref[0] = k.astype(BF16)
    v_ref[0] = seg(3).astype(BF16)
    gs_ref[0] = _silu(seg(4)).astype(BF16)


def _odd_in_call(x, g, shift, scale, w, clb, layer):
    bsz, s, d = x.shape
    hk = clb.shape[-1]
    tm = min(PROJ_ROWS, s)
    row = lambda b, i: (b, i, 0)
    vec = lambda b, i: (b, 0, 0)
    out = lambda dt: jax.ShapeDtypeStruct((bsz, s, hk), dt)
    return pl.pallas_call(
        functools.partial(_odd_in_kernel, layer=layer),
        grid=(bsz, s // tm),
        in_specs=[pl.BlockSpec((1, tm, d), row),
                  pl.BlockSpec((1, d), lambda b, i: (0, 0)),
                  pl.BlockSpec((1, 1, d), vec),
                  pl.BlockSpec((1, 1, d), vec),
                  pl.BlockSpec(w.shape, lambda b, i: (0, 0), pipeline_mode=pl.Buffered(1)),
                  pl.BlockSpec(clb.shape, lambda b, i: (0, 0, 0))],
        out_specs=[pl.BlockSpec((1, tm, hk), row)] * 7,
        out_shape=[out(BF16), out(F32), out(F32), out(BF16), out(BF16), out(BF16), out(BF16)],
        compiler_params=_params("arbitrary", "arbitrary"),
        name="odd_in_proj",
    )(x, g.reshape(1, d), shift, scale, w, clb)


def _t5_bucket(rel):
    half = NUM_BUCKETS // 2
    max_exact = half // 2
    n = jnp.abs(rel)
    nf = jnp.maximum(n, 1).astype(F32)
    large = max_exact + (jnp.log(nf / max_exact) / math.log(MAX_DISTANCE / max_exact)
                         * (half - max_exact)).astype(jnp.int32)
    large = jnp.minimum(large, half - 1)
    return jnp.where(rel > 0, half, 0) + jnp.where(n < max_exact, n, large)


def _bias_tiles(rel_bias):
    r = jnp.arange(LANES)[:, None]
    c = jnp.arange(LANES)[None, :]

    def lookup(rel, table):
        hot = jax.nn.one_hot(_t5_bucket(rel), NUM_BUCKETS, dtype=F32)
        return jnp.einsum('...k,kh->h...', hot, table.astype(F32), precision=lax.Precision.HIGHEST)

    rel_a = (jnp.arange(-2, 3) * LANES)[:, None, None] + (c - r)[None]
    rel_b = jnp.arange(3 * LANES)[None, :] - LANES - r
    return lookup(rel_a, rel_bias[:, :A_HEADS]), lookup(rel_b, rel_bias[:, A_HEADS:])


def _diff_attn_kernel(q_ref, k_ref, v_ref, bias_ref, qg_ref, kg_ref, lam_ref, sg_ref, o_ref, *, lam_init, tq):
    s_len = k_ref.shape[1]
    nb = s_len // LANES
    first = lax.broadcasted_iota(jnp.int32, (1, LANES), 1) < HEAD_DIM
    kn = _norm_halves(k_ref[0].astype(F32), kg_ref[...], first).astype(BF16)
    v = v_ref[0]

    lp = lam_ref[...]
    lam = (jnp.exp(jnp.sum(lp[0:1] * lp[1:2], axis=-1, keepdims=True))
           - jnp.exp(jnp.sum(lp[2:3] * lp[3:4], axis=-1, keepdims=True)) + lam_init)

    def add_bias(sc, first_block):
        rows = []
        for a in range(tq // LANES):
            tiles = []
            for j in range(nb):
                d = j - (first_block + a)
                tile = sc[a * LANES:(a + 1) * LANES, j * LANES:(j + 1) * LANES]
                tiles.append(tile if d <= -2 else tile + bias_ref[0, min(d, 2) + 2])
            rows.append(jnp.concatenate(tiles, axis=1))
        return jnp.concatenate(rows, axis=0)

    for t in range(s_len // tq):
        rows = slice(t * tq, (t + 1) * tq)
        qn = _norm_halves(q_ref[0, rows, :].astype(F32), qg_ref[...], first) * (HEAD_DIM ** -0.5 * LOG2E)
        qms = [jnp.where(first, qn, 0.0).astype(BF16), jnp.where(first, 0.0, qn).astype(BF16)]
        scs = [add_bias(lax.dot_general(qm, kn, _TRANS_B, preferred_element_type=F32), t * (tq // LANES))
               for qm in qms]
        es = [jnp.exp2(sc - jnp.max(sc, axis=-1, keepdims=True)) for sc in scs]
        rs = [1.0 / jnp.sum(e, axis=-1, keepdims=True) for e in es]
        evs = [_dot(e.astype(BF16), v) for e in es]
        o = evs[0] * rs[0] - evs[1] * (lam * rs[1])
        o = o * lax.rsqrt(jnp.mean(o * o, axis=-1, keepdims=True) + EPS) * sg_ref[...] * (1.0 - lam_init)
        o_ref[0, rows, :] = o.astype(BF16)


def _diff_attn_call(qkv, tiles_a, qk_g, lam_p, subln_g, lam_init):
    bsz, s, _ = qkv.shape
    dup = lambda g: jnp.concatenate([g, g]).reshape(1, LANES)
    col = lambda blk: pl.BlockSpec((1, s, LANES), lambda b, h: (b, 0, blk * A_HEADS + h))
    one = pl.BlockSpec((1, LANES), lambda b, h: (0, 0))
    rel_tiles = (tiles_a - tiles_a[:, :1]) * LOG2E
    return pl.pallas_call(
        functools.partial(_diff_attn_kernel, lam_init=lam_init, tq=min(DIFF_ATTN_ROWS, s)),
        grid=(bsz, A_HEADS),
        in_specs=[col(0), col(1), col(2),
                  pl.BlockSpec((1, 5, LANES, LANES), lambda b, h: (h, 0, 0, 0)),
                  one, one, pl.BlockSpec(lam_p.shape, lambda b, h: (0, 0)), one],
        out_specs=col(0),
        out_shape=jax.ShapeDtypeStruct((bsz, s, A_HEADS * LANES), BF16),
        compiler_params=_params("arbitrary", "arbitrary"),
        name="diff_attn",
    )(qkv, qkv, qkv, rel_tiles, dup(qk_g[0]), dup(qk_g[1]), lam_p, subln_g.reshape(1, LANES))


def _win_attn_kernel(q_ref, k_ref, v_ref, bias_ref, qg_ref, kg_ref, sink_ref, o_ref, kn_scr, *, q_blocks):
    i = pl.program_id(1)
    nb = pl.num_programs(1) * q_blocks
    group = B_Q_HEADS // B_KV_HEADS
    first = lax.broadcasted_iota(jnp.int32, (1, LANES), 1) < HEAD_DIM

    @pl.when(i == 0)
    def _():
        kn_scr[...] = _norm_halves(k_ref[0].astype(F32), kg_ref[...], first).astype(BF16)

    r = lax.broadcasted_iota(jnp.int32, (group * LANES, 1), 0) & (LANES - 1)
    c = lax.broadcasted_iota(jnp.int32, (1, 3 * LANES), 1)
    band = jnp.abs(c - LANES - r) <= WINDOW
    sinks = [jnp.max(sink_ref[kv], axis=-1, keepdims=True) for kv in range(B_KV_HEADS)]

    for u in range(q_blocks):
        blk = i * q_blocks + u
        rows = slice(u * LANES, (u + 1) * LANES)
        starts = [pl.multiple_of(j * LANES, LANES)
                  for j in (jnp.maximum(blk - 1, 0), blk, jnp.minimum(blk + 1, nb - 1))]
        k3 = jnp.concatenate([kn_scr[pl.ds(st, LANES), :] for st in starts], axis=0)
        v3 = jnp.concatenate([v_ref[0, pl.ds(st, LANES), :] for st in starts], axis=0)
        valid = band & ((c >= LANES) | (blk > 0)) & ((c < 2 * LANES) | (blk < nb - 1))

        pairs = []
        for pb in range(B_Q_HEADS // 2):
            qp = _norm_halves(q_ref[0, rows, pb * LANES:(pb + 1) * LANES].astype(F32), qg_ref[...], first)
            qp = qp * (HEAD_DIM ** -0.5 * LOG2E)
            pairs.append((qp, pltpu.roll(qp, HEAD_DIM, 1)))

        for kv in range(B_KV_HEADS):
            kv_lanes = first if kv == 0 else jnp.logical_not(first)
            pbs = range(kv * group // 2, (kv + 1) * group // 2)
            qs = jnp.concatenate([jnp.where(kv_lanes, pairs[pb][0 if a == kv else 1], 0.0).astype(BF16)
                                  for pb in pbs for a in range(2)], axis=0)
            sc = lax.dot_general(qs, k3, _TRANS_B, preferred_element_type=F32) + bias_ref[kv]
            sc = jnp.where(valid, sc, -jnp.inf)
            sk = sinks[kv]
            m = jnp.maximum(jnp.max(sc, axis=-1, keepdims=True), sk)
            e = jnp.exp2(sc - m)
            p = e * (1.0 / (jnp.sum(e, axis=-1, keepdims=True) + jnp.exp2(sk - m)))
            o = _dot(p.astype(BF16), v3)
            for n, pb in enumerate(pbs):
                halves = []
                for a in range(2):
                    oh = o[(2 * n + a) * LANES:(2 * n + a + 1) * LANES]
                    halves.append(oh if a == kv else pltpu.roll(oh, HEAD_DIM, 1))
                o_ref[0, rows, pb * LANES:(pb + 1) * LANES] = jnp.where(first, halves[0], halves[1]).astype(BF16)


def _win_attn_call(qkv, tiles_b, qk_g, sink):
    bsz, s, n = qkv.shape
    qw = B_Q_HEADS * HEAD_DIM
    group = B_Q_HEADS // B_KV_HEADS
    q_blocks = min(WIN_ATTN_BLOCKS, s // LANES)
    tq = q_blocks * LANES
    q_blk = (2 * A_HEADS * LANES + A_HEADS * LANES) // qw
    k_blk = (2 * A_HEADS * LANES + A_HEADS * LANES + qw) // LANES
    dup = lambda g: jnp.concatenate([g, g]).reshape(1, LANES)
    bias = tiles_b.reshape(B_KV_HEADS, group * LANES, 3 * LANES) * LOG2E
    sink_rows = jnp.broadcast_to(jnp.repeat(sink.astype(F32).reshape(B_KV_HEADS, group) * LOG2E, LANES, axis=1)[..., None],
                                 (B_KV_HEADS, group * LANES, LANES))
    return pl.pallas_call(
        functools.partial(_win_attn_kernel, q_blocks=q_blocks),
        grid=(bsz, s // tq),
        in_specs=[pl.BlockSpec((1, tq, qw), lambda b, i: (b, i, q_blk)),
                  pl.BlockSpec((1, s, LANES), lambda b, i: (b, 0, k_blk)),
                  pl.BlockSpec((1, s, LANES), lambda b, i: (b, 0, k_blk + 1)),
                  pl.BlockSpec(bias.shape, lambda b, i: (0, 0, 0)),
                  pl.BlockSpec((1, LANES), lambda b, i: (0, 0)),
                  pl.BlockSpec((1, LANES), lambda b, i: (0, 0)),
                  pl.BlockSpec(sink_rows.shape, lambda b, i: (0, 0, 0))],
        out_specs=pl.BlockSpec((1, tq, qw), lambda b, i: (b, i, 0)),
        out_shape=jax.ShapeDtypeStruct((bsz, s, qw), BF16),
        scratch_shapes=[pltpu.VMEM((s, LANES), BF16)],
        compiler_params=_params("arbitrary", "arbitrary"),
        name="win_attn",
    )(qkv, qkv, qkv, bias, dup(qk_g[2]), dup(qk_g[3]), sink_rows)


def _gla_constants(cs):
    t = np.arange(cs)[:, None]
    u = np.arange(cs)[None, :]
    sums, pairs = [], []
    for rev in (False, True):
        mats = [(u >= t) if rev else (u <= t)]
        for h in (2, 4):
            start = (t // (2 * h)) * 2 * h
            later = (t // h) % 2 == 1
            if rev:
                b = start + h
                mats.append(np.where(later, (u >= b) & (u < t), (u >= t) & (u < b)))
            else:
                b = start + h - 1
                mats.append(np.where(later, (u > b) & (u <= t), (u > t) & (u <= b)))
        sums.append(np.concatenate(mats, axis=0))
        lv, h = [], 1
        while h < cs:
            same = (t // (2 * h)) == (u // (2 * h))
            t_later = (t // h) % 2 == 1
            u_later = (u // h) % 2 == 1
            lv.append(same & (t_later != rev) & (u_later == rev))
            h *= 2
        pairs.append(np.stack(lv))
    return jnp.asarray(np.stack(sums), BF16), jnp.asarray(np.stack(pairs), F32)


def _gla_chunks(chains, pairs_ref):
    cs = chains[0][0].shape[0]
    odd_row = (lax.broadcasted_iota(jnp.int32, (cs, 1), 0) & 1) == 1
    ex_alls = []
    for q, k, v, lf2, st, sums, rev in chains:
        hi = lf2.astype(BF16)
        lo = (lf2 - hi.astype(F32)).astype(BF16)
        ex_alls.append(_dot(sums, hi) + _dot(sums, lo))

    outs, states, gcs = [], [], []
    for (q, k, v, lf2, st, sums, rev), ex_all in zip(chains, ex_alls):
        if isinstance(st, int):
            st = states[st]
        gc = ex_all[0:cs]
        tot = gc[0:1] if rev else gc[cs - 1:cs]
        o = lax.dot_general((q * jnp.exp2(gc)).astype(BF16), st.astype(BF16), _TRANS_B, preferred_element_type=F32)
        kd = (k * jnp.exp2(tot - gc)).astype(BF16)
        states.append(jnp.exp2(tot) * st + lax.dot_general(v, kd, _TRANS_A, preferred_element_type=F32))
        outs.append(o + jnp.sum(q * k, axis=-1, keepdims=True) * v.astype(F32))
        gcs.append(gc)

    attns = [jnp.zeros((cs, cs), F32) for _ in chains]
    for level in range(cs.bit_length() - 1):
        h = 1 << level
        for n, ((q, k, v, lf2, st, sums, rev), ex_all, gc) in enumerate(zip(chains, ex_alls, gcs)):
            if h == 1:
                ex = jnp.where(odd_row != rev, lf2, 0.0)
            elif h < SUBLANES:
                ex = ex_all[level * cs:(level + 1) * cs]
            else:
                bounds = [p * 2 * h + (h if rev else h - 1) for p in range(cs // (2 * h))]
                ref = jnp.concatenate([jnp.broadcast_to(gc[b:b + 1], (2 * h, gc.shape[1])) for b in bounds], axis=0)
                d = gc - ref
                ex = jnp.minimum(d, -d)
            x = jnp.exp2(ex)
            pr = lax.dot_general((q * x).astype(BF16), (k * x).astype(BF16), _TRANS_B, preferred_element_type=F32)
            attns[n] = attns[n] + pr * pairs_ref[int(rev), level]
    return [(o + _dot(attn.astype(BF16), ch[2]), st) for o, attn, ch, st in zip(outs, attns, chains, states)]


def _gla_kernel(qs_ref, lff_ref, lfb_ref, kf_ref, kb_ref, v_ref, gs_ref, og_ref, sums_ref, pairs_ref, y_ref,
                of_scr, ob_scr):
    s_len = qs_ref.shape[1]
    heads = qs_ref.shape[2] // LANES
    cs = sums_ref.shape[2]
    nc = s_len // cs
    specs = [(j, rev, lf_ref, k_ref, o_scr) for j in range(heads)
             for rev, lf_ref, k_ref, o_scr in ((False, lff_ref, kf_ref, of_scr), (True, lfb_ref, kb_ref, ob_scr))]

    n = len(specs)

    def body(c, states):
        where, chains = [], []
        for step in range(GLA_CHUNKS_PER_STEP):
            for m, (j, rev, lf_ref, k_ref, _) in enumerate(specs):
                chunk = c * GLA_CHUNKS_PER_STEP + step
                r = pl.ds(pl.multiple_of(((nc - 1 - chunk) if rev else chunk) * cs, cs), cs)
                l = slice(j * LANES, (j + 1) * LANES)
                where.append((r, l))
                chains.append((qs_ref[0, r, l].astype(F32), k_ref[0, r, l].astype(F32), v_ref[0, r, l],
                               lf_ref[0, r, l], states[m] if step == 0 else (step - 1) * n + m,
                               sums_ref[int(rev)], rev))
        outs = _gla_chunks(chains, pairs_ref)
        for (r, l), (o, _), (*_, o_scr) in zip(where, outs, specs * GLA_CHUNKS_PER_STEP):
            o_scr[r, l] = o
        return tuple(st for _, st in outs[-n:])

    zero = jnp.zeros((LANES, LANES), F32)
    lax.fori_loop(0, nc // GLA_CHUNKS_PER_STEP, body, (zero,) * n, unroll=True)
    for j in range(heads):
        l = slice(j * LANES, (j + 1) * LANES)
        o = of_scr[:, l] + ob_scr[:, l]
        y = o * lax.rsqrt(jnp.mean(o * o, axis=-1, keepdims=True) + EPS) * og_ref[...]
        y_ref[0, :, l] = (y * gs_ref[0, :, l].astype(F32)).astype(BF16)


def _gla_call(qs, lff, lfb, kf, kb, v, gs, out_g):
    bsz, s, hk = qs.shape
    sums, pairs = _gla_constants(min(GLA_CHUNK, s))
    width = GLA_HEADS_PER_STEP * LANES
    blk = pl.BlockSpec((1, s, width), lambda b, h: (b, 0, h))
    return pl.pallas_call(
        _gla_kernel,
        grid=(bsz, hk // width),
        in_specs=[blk] * 7 + [pl.BlockSpec((1, LANES), lambda b, h: (0, 0)),
                              pl.BlockSpec(sums.shape, lambda b, h: (0, 0, 0)),
                              pl.BlockSpec(pairs.shape, lambda b, h: (0, 0, 0, 0))],
        out_specs=blk,
        out_shape=jax.ShapeDtypeStruct((bsz, s, hk), BF16),
        scratch_shapes=[pltpu.VMEM((s, width), F32), pltpu.VMEM((s, width), F32)],
        compiler_params=_params("arbitrary", "arbitrary"),
        name="gla",
    )(qs, lff, lfb, kf, kb, v, gs, out_g.reshape(1, LANES), sums, pairs)


def kernel(x, c, ada_w, ada_b, norm_g, ffn_up, ffn_down, even_w_in, even_w_out, qk_norm_g, diff_lambda,
           diff_subln_g, sink_logit, rel_bias, odd_w_in, odd_w_out, c_lower_bound, c_out_norm_g):
    bsz, s, d = x.shape
    depth = ada_w.shape[0]
    assert d == 2 * A_HEADS * LANES == B_Q_HEADS * LANES == C_HEADS * LANES
    assert s % (GLA_CHUNK * GLA_CHUNKS_PER_STEP) == 0 and s % min(DIFF_ATTN_ROWS, s) == 0

    mod_all = _mod_call(c, ada_w, ada_b)
    tiles_a, tiles_b = _bias_tiles(rel_bias)
    for l in range(depth):
        mod = mod_all[l].reshape(bsz, N_MOD, 1, d)
        sh1, sc1, g1, sh2, sc2, g2, sh3, sc3, g3 = (mod[:, k] for k in range(N_MOD))
        x = _ffn_call(x, norm_g[l, 0], sh1, sc1, g1, ffn_up[l, 0].astype(BF16), ffn_down[l, 0].astype(BF16))
        if l % 2 == 0:
            e = l // 2
            lam_init = 0.8 - 0.6 * math.exp(-0.3 * l)
            qkv = _even_in_call(x, norm_g[l, 1], sh2, sc2, even_w_in[e].astype(BF16))
            ya = _diff_attn_call(qkv, tiles_a, qk_norm_g[e], diff_lambda[e].astype(F32), diff_subln_g[e], lam_init)
            yb = _win_attn_call(qkv, tiles_b, qk_norm_g[e], sink_logit[e])
            mixer = (g2, [ya, yb], even_w_out[e].astype(BF16))
        else:
            o = l // 2
            parts = _odd_in_call(x, norm_g[l, 1], sh2, sc2, odd_w_in[o].astype(BF16),
                                 c_lower_bound.astype(F32), l)
            mixer = (g2, [_gla_call(*parts, c_out_norm_g[o])], odd_w_out[o].astype(BF16))
        x = _ffn_call(x, norm_g[l, 2], sh3, sc3, g3, ffn_up[l, 1].astype(BF16), ffn_down[l, 1].astype(BF16), mixer)
    return x
```

```python
import functools
import math

import jax
import jax.numpy as jnp
import numpy as np
from jax import lax
from jax.experimental import pallas as pl
from jax.experimental.pallas import tpu as pltpu

F32 = jnp.float32
BF16 = jnp.bfloat16

EPS = 1e-6
N_MOD = 9
HEAD_DIM = 64
LANES = 128
A_HEADS = 4
B_Q_HEADS = 8
B_KV_HEADS = 2
WINDOW = 128
NUM_BUCKETS = 32
MAX_DISTANCE = 128
C_HEADS = 8
FFN_CHUNK = 1024
FFN_ROWS = 1024
PROJ_ROWS = 512
MOD_COLS = 1152
DIFF_ATTN_ROWS = 512
WIN_ATTN_BLOCKS = 4
SUBLANES = 8
GLA_CHUNK = 128
GLA_CHUNKS_PER_STEP = 2
GLA_HEADS_PER_STEP = 2
LOG2E = math.log2(math.e)
V7X_VMEM_LIMIT_BYTES = 56 * 1024 * 1024

_TRANS_B = (((1,), (1,)), ((), ()))
_TRANS_A = (((0,), (0,)), ((), ()))


def _params(*sem):
    return pltpu.CompilerParams(dimension_semantics=sem, vmem_limit_bytes=V7X_VMEM_LIMIT_BYTES)


def _dot(a, b):
    return jnp.dot(a, b, preferred_element_type=F32)


def _silu(t):
    u = 0.5 * t
    return u + u * jnp.tanh(u)


def _norm_mod(x, g, shift, scale):
    ms = jnp.mean(x * x, axis=-1, keepdims=True)
    y = x * lax.rsqrt(ms + EPS) * g
    return y * (1.0 + scale) + shift


def _norm_halves(x, g, first):
    sq = x * x
    s0 = jnp.sum(jnp.where(first, sq, 0.0), axis=-1, keepdims=True)
    s1 = jnp.sum(jnp.where(first, 0.0, sq), axis=-1, keepdims=True)
    ms = jnp.where(first, s0, s1) * (1.0 / HEAD_DIM)
    return x * lax.rsqrt(ms + EPS) * g


def _mod_kernel(c_ref, w_ref, b_ref, o_ref):
    sc = _silu(c_ref[...])
    o_ref[0] = jnp.dot(sc, w_ref[0], precision=lax.Precision.HIGHEST,
                       preferred_element_type=F32) + b_ref[0]


def _mod_call(c, ada_w, ada_b):
    depth, d, m = ada_w.shape
    bsz = c.shape[0]
    tn = MOD_COLS
    assert m % tn == 0
    return pl.pallas_call(
        _mod_kernel,
        grid=(depth, m // tn),
        in_specs=[pl.BlockSpec((bsz, d), lambda l, j: (0, 0)),
                  pl.BlockSpec((1, d, tn), lambda l, j: (l, 0, j)),
                  pl.BlockSpec((1, 1, tn), lambda l, j: (l, 0, j))],
        out_specs=pl.BlockSpec((1, bsz, tn), lambda l, j: (l, 0, j)),
        out_shape=jax.ShapeDtypeStruct((depth, bsz, m), F32),
        compiler_params=_params("arbitrary", "arbitrary"),
        name="adaln_mod",
    )(c, ada_w, ada_b.reshape(depth, 1, m))


def _ffn_kernel(x_ref, g_ref, sh_ref, sc_ref, gate_ref, wu_ref, wd_ref, *rest, chunks):
    o_ref = rest[-1]
    x = x_ref[0]
    if len(rest) > 1:
        n_in = (len(rest) - 2) // 2
        mix = _dot(rest[1][0], rest[1 + n_in][...])
        for y_ref, w_ref in zip(rest[2:1 + n_in], rest[2 + n_in:1 + 2 * n_in]):
            mix += _dot(y_ref[0], w_ref[...])
        x = x + rest[0][0] * mix
    h = _norm_mod(x, g_ref[...], sh_ref[0], sc_ref[0]).astype(BF16)
    f = wd_ref.shape[0]
    acc = None
    for c0, cw in chunks:
        a = _dot(h, wu_ref[:, c0:c0 + cw])
        b = _dot(h, wu_ref[:, f + c0:f + c0 + cw])
        part = _dot((_silu(a) * b).astype(BF16), wd_ref[c0:c0 + cw, :])
        acc = part if acc is None else acc + part
    o_ref[0] = x + 0.5 * gate_ref[0] * acc


def _ffn_call(x, g, shift, scale, gate, w_up, w_down, which, mixer=None):
    bsz, s, d = x.shape
    f = w_down.shape[2]
    tm = min(FFN_ROWS, s)
    chunks = tuple((c0, min(FFN_CHUNK, f - c0)) for c0 in range(0, f, FFN_CHUNK))
    row = lambda b, i: (b, i, 0)
    vec = lambda b, i: (b, 0, 0)
    resident = dict(pipeline_mode=pl.Buffered(1))
    in_specs = [pl.BlockSpec((1, tm, d), row),
                pl.BlockSpec((1, d), lambda b, i: (0, 0)),
                pl.BlockSpec((1, 1, d), vec),
                pl.BlockSpec((1, 1, d), vec),
                pl.BlockSpec((1, 1, d), vec),
                pl.BlockSpec((None, None) + w_up.shape[2:], lambda b, i: which + (0, 0), **resident),
                pl.BlockSpec((None, None) + w_down.shape[2:], lambda b, i: which + (0, 0), **resident)]
    args = [x, g.reshape(1, d), shift, scale, gate, w_up, w_down]
    if mixer is not None:
        mixer_gate, ys, w_out = mixer
        wk = ys[0].shape[-1]
        in_specs += ([pl.BlockSpec((1, 1, d), vec)] + [pl.BlockSpec((1, tm, wk), row)] * len(ys)
                     + [pl.BlockSpec((wk, d), functools.partial(lambda b, i, k: (k, 0), k=k), **resident)
                        for k in range(len(ys))])
        args += [mixer_gate, *ys, *([w_out] * len(ys))]
    return pl.pallas_call(
        functools.partial(_ffn_kernel, chunks=chunks),
        grid=(bsz, s // tm),
        in_specs=in_specs,
        out_specs=pl.BlockSpec((1, tm, d), row),
        out_shape=jax.ShapeDtypeStruct(x.shape, F32),
        compiler_params=_params("arbitrary", "arbitrary"),
        name="ffn",
    )(*args)


def _even_in_kernel(x_ref, g_ref, sh_ref, sc_ref, w_ref, o_ref):
    h = _norm_mod(x_ref[0], g_ref[...], sh_ref[0], sc_ref[0]).astype(BF16)
    o_ref[0] = _dot(h, w_ref[...]).astype(BF16)


def _even_in_call(x, g, shift, scale, w):
    bsz, s, d = x.shape
    n = w.shape[1]
    tm = min(PROJ_ROWS, s)
    row = lambda b, i: (b, i, 0)
    vec = lambda b, i: (b, 0, 0)
    return pl.pallas_call(
        _even_in_kernel,
        grid=(bsz, s // tm),
        in_specs=[pl.BlockSpec((1, tm, d), row),
                  pl.BlockSpec((1, d), lambda b, i: (0, 0)),
                  pl.BlockSpec((1, 1, d), vec),
                  pl.BlockSpec((1, 1, d), vec),
                  pl.BlockSpec((d, n), lambda b, i: (0, 0))],
        out_specs=pl.BlockSpec((1, tm, n), row),
        out_shape=jax.ShapeDtypeStruct((bsz, s, n), BF16),
        compiler_params=_params("arbitrary", "arbitrary"),
        name="even_in_proj",
    )(x, g.reshape(1, d), shift, scale, w)


def _log_forget_and_input_gate(f_raw, clb, layer):
    e = jnp.exp(clb - jnp.max(clb, axis=0, keepdims=True))
    lb = jnp.sum(e[1:layer + 1], axis=0, keepdims=True) / jnp.sum(e, axis=0, keepdims=True)
    half_th = 0.5 * jnp.tanh(0.5 * f_raw)
    f = lb + (1.0 - lb) * (0.5 + half_th)
    lf2 = jnp.where(f > 0.0, jnp.log2(f), (jnp.log(1.0 - lb) + f_raw) * LOG2E)
    return lf2, (1.0 - lb) * (0.5 - half_th)


def _odd_in_kernel(x_ref, g_ref, sh_ref, sc_ref, w_ref, clb_ref,
                   qs_ref, lff_ref, lfb_ref, kf_ref, kb_ref, v_ref, gs_ref, *, layer):
    h = _norm_mod(x_ref[0], g_ref[...], sh_ref[0], sc_ref[0]).astype(BF16)
    hk = qs_ref.shape[-1]
    seg = lambda n: _dot(h, w_ref[:, n * hk:(n + 1) * hk])
    qs_ref[0] = _silu(seg(0)).astype(BF16)
    for n, lf_ref, k_ref in ((1, lff_ref, kf_ref), (2, lfb_ref, kb_ref)):
        lf2, k = _log_forget_and_input_gate(seg(n), clb_ref[n - 1], layer)
        lf_ref[0] = lf2
        k_ref[0] = k.astype(BF16)
    v_ref[0] = seg(3).astype(BF16)
    gs_ref[0] = _silu(seg(4)).astype(BF16)


def _odd_in_call(x, g, shift, scale, w, clb, layer):
    bsz, s, d = x.shape
    hk = clb.shape[-1]
    tm = min(PROJ_ROWS, s)
    row = lambda b, i: (b, i, 0)
    vec = lambda b, i: (b, 0, 0)
    out = lambda dt: jax.ShapeDtypeStruct((bsz, s, hk), dt)
    return pl.pallas_call(
        functools.partial(_odd_in_kernel, layer=layer),
        grid=(bsz, s // tm),
        in_specs=[pl.BlockSpec((1, tm, d), row),
                  pl.BlockSpec((1, d), lambda b, i: (0, 0)),
                  pl.BlockSpec((1, 1, d), vec),
                  pl.BlockSpec((1, 1, d), vec),
                  pl.BlockSpec(w.shape, lambda b, i: (0, 0), pipeline_mode=pl.Buffered(1)),
                  pl.BlockSpec(clb.shape, lambda b, i: (0, 0, 0))],
        out_specs=[pl.BlockSpec((1, tm, hk), row)] * 7,
        out_shape=[out(BF16), out(F32), out(F32), out(BF16), out(BF16), out(BF16), out(BF16)],
        compiler_params=_params("arbitrary", "arbitrary"),
        name="odd_in_proj",
    )(x, g.reshape(1, d), shift, scale, w, clb)


def _t5_bucket(rel):
    half = NUM_BUCKETS // 2
    max_exact = half // 2
    n = jnp.abs(rel)
    nf = jnp.maximum(n, 1).astype(F32)
    large = max_exact + (jnp.log(nf / max_exact) / math.log(MAX_DISTANCE / max_exact)
                         * (half - max_exact)).astype(jnp.int32)
    large = jnp.minimum(large, half - 1)
    return jnp.where(rel > 0, half, 0) + jnp.where(n < max_exact, n, large)


def _bias_tiles(rel_bias):
    r = jnp.arange(LANES)[:, None]
    c = jnp.arange(LANES)[None, :]

    def lookup(rel, table):
        hot = jax.nn.one_hot(_t5_bucket(rel), NUM_BUCKETS, dtype=F32)
        return jnp.einsum('...k,kh->h...', hot, table.astype(F32), precision=lax.Precision.HIGHEST)

    rel_a = (jnp.arange(-2, 3) * LANES)[:, None, None] + (c - r)[None]
    rel_b = jnp.arange(3 * LANES)[None, :] - LANES - r
    return lookup(rel_a, rel_bias[:, :A_HEADS]), lookup(rel_b, rel_bias[:, A_HEADS:])


def _diff_attn_kernel(q_ref, k_ref, v_ref, bias_ref, qg_ref, kg_ref, lam_ref, sg_ref, o_ref, *, lam_init, tq):
    s_len = k_ref.shape[1]
    nb = s_len // LANES
    first = lax.broadcasted_iota(jnp.int32, (1, LANES), 1) < HEAD_DIM
    kn = _norm_halves(k_ref[0].astype(F32), kg_ref[...], first).astype(BF16)
    v = v_ref[0]

    lp = lam_ref[...]
    lam = (jnp.exp(jnp.sum(lp[0:1] * lp[1:2], axis=-1, keepdims=True))
           - jnp.exp(jnp.sum(lp[2:3] * lp[3:4], axis=-1, keepdims=True)) + lam_init)

    def add_bias(sc, first_block):
        rows = []
        for a in range(tq // LANES):
            tiles = []
            for j in range(nb):
                d = j - (first_block + a)
                tile = sc[a * LANES:(a + 1) * LANES, j * LANES:(j + 1) * LANES]
                tiles.append(tile if d <= -2 else tile + bias_ref[0, min(d, 2) + 2])
            rows.append(jnp.concatenate(tiles, axis=1))
        return jnp.concatenate(rows, axis=0)

    for t in range(s_len // tq):
        rows = slice(t * tq, (t + 1) * tq)
        qn = _norm_halves(q_ref[0, rows, :].astype(F32), qg_ref[...], first) * (HEAD_DIM ** -0.5 * LOG2E)
        qms = [jnp.where(first, qn, 0.0).astype(BF16), jnp.where(first, 0.0, qn).astype(BF16)]
        scs = [add_bias(lax.dot_general(qm, kn, _TRANS_B, preferred_element_type=F32), t * (tq // LANES))
               for qm in qms]
        es = [jnp.exp2(sc - jnp.max(sc, axis=-1, keepdims=True)) for sc in scs]
        rs = [1.0 / jnp.sum(e, axis=-1, keepdims=True) for e in es]
        evs = [_dot(e.astype(BF16), v) for e in es]
        o = evs[0] * rs[0] - evs[1] * (lam * rs[1])
        o = o * lax.rsqrt(jnp.mean(o * o, axis=-1, keepdims=True) + EPS) * sg_ref[...] * (1.0 - lam_init)
        o_ref[0, rows, :] = o.astype(BF16)


def _diff_attn_call(qkv, tiles_a, qk_g, lam_p, subln_g, lam_init):
    bsz, s, _ = qkv.shape
    dup = lambda g: jnp.concatenate([g, g]).reshape(1, LANES)
    col = lambda blk: pl.BlockSpec((1, s, LANES), lambda b, h: (b, 0, blk * A_HEADS + h))
    one = pl.BlockSpec((1, LANES), lambda b, h: (0, 0))
    rel_tiles = (tiles_a - tiles_a[:, :1]) * LOG2E
    return pl.pallas_call(
        functools.partial(_diff_attn_kernel, lam_init=lam_init, tq=min(DIFF_ATTN_ROWS, s)),
        grid=(bsz, A_HEADS),
        in_specs=[col(0), col(1), col(2),
                  pl.BlockSpec((1, 5, LANES, LANES), lambda b, h: (h, 0, 0, 0)),
                  one, one, pl.BlockSpec(lam_p.shape, lambda b, h: (0, 0)), one],
        out_specs=col(0),
        out_shape=jax.ShapeDtypeStruct((bsz, s, A_HEADS * LANES), BF16),
        compiler_params=_params("arbitrary", "arbitrary"),
        name="diff_attn",
    )(qkv, qkv, qkv, rel_tiles, dup(qk_g[0]), dup(qk_g[1]), lam_p, subln_g.reshape(1, LANES))


def _win_attn_kernel(q_ref, k_ref, v_ref, bias_ref, qg_ref, kg_ref, sink_ref, o_ref, kn_scr, *, q_blocks):
    i = pl.program_id(1)
    nb = pl.num_programs(1) * q_blocks
    group = B_Q_HEADS // B_KV_HEADS
    first = lax.broadcasted_iota(jnp.int32, (1, LANES), 1) < HEAD_DIM

    @pl.when(i == 0)
    def _():
        kn_scr[...] = _norm_halves(k_ref[0].astype(F32), kg_ref[...], first).astype(BF16)

    r = lax.broadcasted_iota(jnp.int32, (group * LANES, 1), 0) & (LANES - 1)
    c = lax.broadcasted_iota(jnp.int32, (1, 3 * LANES), 1)
    band = jnp.abs(c - LANES - r) <= WINDOW
    sinks = [jnp.max(sink_ref[kv], axis=-1, keepdims=True) for kv in range(B_KV_HEADS)]

    for u in range(q_blocks):
        blk = i * q_blocks + u
        rows = slice(u * LANES, (u + 1) * LANES)
        starts = [pl.multiple_of(j * LANES, LANES)
                  for j in (jnp.maximum(blk - 1, 0), blk, jnp.minimum(blk + 1, nb - 1))]
        k3 = jnp.concatenate([kn_scr[pl.ds(st, LANES), :] for st in starts], axis=0)
        v3 = jnp.concatenate([v_ref[0, pl.ds(st, LANES), :] for st in starts], axis=0)
        valid = band & ((c >= LANES) | (blk > 0)) & ((c < 2 * LANES) | (blk < nb - 1))

        pairs = []
        for pb in range(B_Q_HEADS // 2):
            qp = _norm_halves(q_ref[0, rows, pb * LANES:(pb + 1) * LANES].astype(F32), qg_ref[...], first)
            qp = qp * (HEAD_DIM ** -0.5 * LOG2E)
            pairs.append((qp, pltpu.roll(qp, HEAD_DIM, 1)))

        for kv in range(B_KV_HEADS):
            kv_lanes = first if kv == 0 else jnp.logical_not(first)
            pbs = range(kv * group // 2, (kv + 1) * group // 2)
            qs = jnp.concatenate([jnp.where(kv_lanes, pairs[pb][0 if a == kv else 1], 0.0).astype(BF16)
                                  for pb in pbs for a in range(2)], axis=0)
            sc = lax.dot_general(qs, k3, _TRANS_B, preferred_element_type=F32) + bias_ref[kv]
            sc = jnp.where(valid, sc, -jnp.inf)
            sk = sinks[kv]
            m = jnp.maximum(jnp.max(sc, axis=-1, keepdims=True), sk)
            e = jnp.exp2(sc - m)
            p = e * (1.0 / (jnp.sum(e, axis=-1, keepdims=True) + jnp.exp2(sk - m)))
            o = _dot(p.astype(BF16), v3)
            for n, pb in enumerate(pbs):
                halves = []
                for a in range(2):
                    oh = o[(2 * n + a) * LANES:(2 * n + a + 1) * LANES]
                    halves.append(oh if a == kv else pltpu.roll(oh, HEAD_DIM, 1))
                o_ref[0, rows, pb * LANES:(pb + 1) * LANES] = jnp.where(first, halves[0], halves[1]).astype(BF16)


def _win_attn_call(qkv, tiles_b, qk_g, sink):
    bsz, s, n = qkv.shape
    qw = B_Q_HEADS * HEAD_DIM
    group = B_Q_HEADS // B_KV_HEADS
    q_blocks = min(WIN_ATTN_BLOCKS, s // LANES)
    tq = q_blocks * LANES
    q_blk = (2 * A_HEADS * LANES + A_HEADS * LANES) // qw
    k_blk = (2 * A_HEADS * LANES + A_HEADS * LANES + qw) // LANES
    dup = lambda g: jnp.concatenate([g, g]).reshape(1, LANES)
    bias = tiles_b.reshape(B_KV_HEADS, group * LANES, 3 * LANES) * LOG2E
    sink_rows = jnp.broadcast_to(jnp.repeat(sink.astype(F32).reshape(B_KV_HEADS, group) * LOG2E, LANES, axis=1)[..., None],
                                 (B_KV_HEADS, group * LANES, LANES))
    return pl.pallas_call(
        functools.partial(_win_attn_kernel, q_blocks=q_blocks),
        grid=(bsz, s // tq),
        in_specs=[pl.BlockSpec((1, tq, qw), lambda b, i: (b, i, q_blk)),
                  pl.BlockSpec((1, s, LANES), lambda b, i: (b, 0, k_blk)),
                  pl.BlockSpec((1, s, LANES), lambda b, i: (b, 0, k_blk + 1)),
                  pl.BlockSpec(bias.shape, lambda b, i: (0, 0, 0)),
                  pl.BlockSpec((1, LANES), lambda b, i: (0, 0)),
                  pl.BlockSpec((1, LANES), lambda b, i: (0, 0)),
                  pl.BlockSpec(sink_rows.shape, lambda b, i: (0, 0, 0))],
        out_specs=pl.BlockSpec((1, tq, qw), lambda b, i: (b, i, 0)),
        out_shape=jax.ShapeDtypeStruct((bsz, s, qw), BF16),
        scratch_shapes=[pltpu.VMEM((s, LANES), BF16)],
        compiler_params=_params("arbitrary", "arbitrary"),
        name="win_attn",
    )(qkv, qkv, qkv, bias, dup(qk_g[2]), dup(qk_g[3]), sink_rows)


def _gla_constants(cs):
    t = np.arange(cs)[:, None]
    u = np.arange(cs)[None, :]
    sums, pairs = [], []
    for rev in (False, True):
        mats = [(u >= t) if rev else (u <= t)]
        for h in (2, 4):
            start = (t // (2 * h)) * 2 * h
            later = (t // h) % 2 == 1
            if rev:
                b = start + h
                mats.append(np.where(later, (u >= b) & (u < t), (u >= t) & (u < b)))
            else:
                b = start + h - 1
                mats.append(np.where(later, (u > b) & (u <= t), (u > t) & (u <= b)))
        sums.append(np.concatenate(mats, axis=0))
        lv, h = [], 1
        while h < cs:
            same = (t // (2 * h)) == (u // (2 * h))
            t_later = (t // h) % 2 == 1
            u_later = (u // h) % 2 == 1
            lv.append(same & (t_later != rev) & (u_later == rev))
            h *= 2
        pairs.append(np.stack(lv))
    return jnp.asarray(np.stack(sums), BF16), jnp.asarray(np.stack(pairs), F32)


def _gla_chunks(chains, pairs_ref):
    cs = chains[0][0].shape[0]
    odd_row = (lax.broadcasted_iota(jnp.int32, (cs, 1), 0) & 1) == 1
    ex_alls = []
    for q, k, v, lf2, st, sums, rev in chains:
        hi = lf2.astype(BF16)
        lo = (lf2 - hi.astype(F32)).astype(BF16)
        ex_alls.append(_dot(sums, hi) + _dot(sums, lo))

    outs, states, gcs = [], [], []
    for (q, k, v, lf2, st, sums, rev), ex_all in zip(chains, ex_alls):
        if isinstance(st, int):
            st = states[st]
        gc = ex_all[0:cs]
        tot = gc[0:1] if rev else gc[cs - 1:cs]
        o = lax.dot_general((q * jnp.exp2(gc)).astype(BF16), st.astype(BF16), _TRANS_B, preferred_element_type=F32)
        kd = (k * jnp.exp2(tot - gc)).astype(BF16)
        states.append(jnp.exp2(tot) * st + lax.dot_general(v, kd, _TRANS_A, preferred_element_type=F32))
        outs.append(o + jnp.sum(q * k, axis=-1, keepdims=True) * v.astype(F32))
        gcs.append(gc)

    attns = [jnp.zeros((cs, cs), F32) for _ in chains]
    for level in range(cs.bit_length() - 1):
        h = 1 << level
        for n, ((q, k, v, lf2, st, sums, rev), ex_all, gc) in enumerate(zip(chains, ex_alls, gcs)):
            if h == 1:
                ex = jnp.where(odd_row != rev, lf2, 0.0)
            elif h < SUBLANES:
                ex = ex_all[level * cs:(level + 1) * cs]
            else:
                bounds = [p * 2 * h + (h if rev else h - 1) for p in range(cs // (2 * h))]
                ref = jnp.concatenate([jnp.broadcast_to(gc[b:b + 1], (2 * h, gc.shape[1])) for b in bounds], axis=0)
                d = gc - ref
                ex = jnp.minimum(d, -d)
            x = jnp.exp2(ex)
            pr = lax.dot_general((q * x).astype(BF16), (k * x).astype(BF16), _TRANS_B, preferred_element_type=F32)
            attns[n] = attns[n] + pr * pairs_ref[int(rev), level]
    return [(o + _dot(attn.astype(BF16), ch[2]), st) for o, attn, ch, st in zip(outs, attns, chains, states)]


def _gla_kernel(qs_ref, lff_ref, lfb_ref, kf_ref, kb_ref, v_ref, gs_ref, og_ref, sums_ref, pairs_ref, y_ref,
                of_scr, ob_scr):
    s_len = qs_ref.shape[1]
    heads = qs_ref.shape[2] // LANES
    cs = sums_ref.shape[2]
    nc = s_len // cs
    specs = [(j, rev, lf_ref, k_ref, o_scr) for j in range(heads)
             for rev, lf_ref, k_ref, o_scr in ((False, lff_ref, kf_ref, of_scr), (True, lfb_ref, kb_ref, ob_scr))]

    n = len(specs)

    def body(c, states):
        where, chains = [], []
        for step in range(GLA_CHUNKS_PER_STEP):
            for m, (j, rev, lf_ref, k_ref, _) in enumerate(specs):
                chunk = c * GLA_CHUNKS_PER_STEP + step
                r = pl.ds(pl.multiple_of(((nc - 1 - chunk) if rev else chunk) * cs, cs), cs)
                l = slice(j * LANES, (j + 1) * LANES)
                where.append((r, l))
                chains.append((qs_ref[0, r, l].astype(F32), k_ref[0, r, l].astype(F32), v_ref[0, r, l],
                               lf_ref[0, r, l], states[m] if step == 0 else (step - 1) * n + m,
                               sums_ref[int(rev)], rev))
        outs = _gla_chunks(chains, pairs_ref)
        for (r, l), (o, _), (*_, o_scr) in zip(where, outs, specs * GLA_CHUNKS_PER_STEP):
            o_scr[r, l] = o
        return tuple(st for _, st in outs[-n:])

    zero = jnp.zeros((LANES, LANES), F32)
    lax.fori_loop(0, nc // GLA_CHUNKS_PER_STEP, body, (zero,) * n, unroll=True)
    for j in range(heads):
        l = slice(j * LANES, (j + 1) * LANES)
        o = of_scr[:, l] + ob_scr[:, l]
        y = o * lax.rsqrt(jnp.mean(o * o, axis=-1, keepdims=True) + EPS) * og_ref[...]
        y_ref[0, :, l] = (y * gs_ref[0, :, l].astype(F32)).astype(BF16)


def _gla_call(qs, lff, lfb, kf, kb, v, gs, out_g):
    bsz, s, hk = qs.shape
    sums, pairs = _gla_constants(min(GLA_CHUNK, s))
    width = GLA_HEADS_PER_STEP * LANES
    blk = pl.BlockSpec((1, s, width), lambda b, h: (b, 0, h))
    return pl.pallas_call(
        _gla_kernel,
        grid=(bsz, hk // width),
        in_specs=[blk] * 7 + [pl.BlockSpec((1, LANES), lambda b, h: (0, 0)),
                              pl.BlockSpec(sums.shape, lambda b, h: (0, 0, 0)),
                              pl.BlockSpec(pairs.shape, lambda b, h: (0, 0, 0, 0))],
        out_specs=blk,
        out_shape=jax.ShapeDtypeStruct((bsz, s, hk), BF16),
        scratch_shapes=[pltpu.VMEM((s, width), F32), pltpu.VMEM((s, width), F32)],
        compiler_params=_params("arbitrary", "arbitrary"),
        name="gla",
    )(qs, lff, lfb, kf, kb, v, gs, out_g.reshape(1, LANES), sums, pairs)


def kernel(x, c, ada_w, ada_b, norm_g, ffn_up, ffn_down, even_w_in, even_w_out, qk_norm_g, diff_lambda,
           diff_subln_g, sink_logit, rel_bias, odd_w_in, odd_w_out, c_lower_bound, c_out_norm_g):
    bsz, s, d = x.shape
    depth = ada_w.shape[0]
    assert d == 2 * A_HEADS * LANES == B_Q_HEADS * LANES == C_HEADS * LANES
    assert s % (GLA_CHUNK * GLA_CHUNKS_PER_STEP) == 0 and s % min(DIFF_ATTN_ROWS, s) == 0

    mod_all = _mod_call(c, ada_w, ada_b)
    tiles_a, tiles_b = _bias_tiles(rel_bias)
    w_up, w_down = ffn_up.astype(BF16), ffn_down.astype(BF16)
    for l in range(depth):
        mod = mod_all[l].reshape(bsz, N_MOD, 1, d)
        sh1, sc1, g1, sh2, sc2, g2, sh3, sc3, g3 = (mod[:, k] for k in range(N_MOD))
        x = _ffn_call(x, norm_g[l, 0], sh1, sc1, g1, w_up, w_down, (l, 0))
        if l % 2 == 0:
            e = l // 2
            lam_init = 0.8 - 0.6 * math.exp(-0.3 * l)
            qkv = _even_in_call(x, norm_g[l, 1], sh2, sc2, even_w_in[e].astype(BF16))
            ya = _diff_attn_call(qkv, tiles_a, qk_norm_g[e], diff_lambda[e].astype(F32), diff_subln_g[e], lam_init)
            yb = _win_attn_call(qkv, tiles_b, qk_norm_g[e], sink_logit[e])
            mixer = (g2, [ya, yb], even_w_out[e].astype(BF16))
        else:
            o = l // 2
            parts = _odd_in_call(x, norm_g[l, 1], sh2, sc2, odd_w_in[o].astype(BF16),
                                 c_lower_bound.astype(F32), l)
            mixer = (g2, [_gla_call(*parts, c_out_norm_g[o])], odd_w_out[o].astype(BF16))
        x = _ffn_call(x, norm_g[l, 2], sh3, sc3, g3, w_up, w_down, (l, 1), mixer)
    return x
```

```python
import functools
import math

import jax
import jax.numpy as jnp
import numpy as np
from jax import lax
from jax.experimental import pallas as pl
from jax.experimental.pallas import tpu as pltpu

F32 = jnp.float32
BF16 = jnp.bfloat16

EPS = 1e-6
N_MOD = 9
HEAD_DIM = 64
LANES = 128
A_HEADS = 4
B_Q_HEADS = 8
B_KV_HEADS = 2
WINDOW = 128
NUM_BUCKETS = 32
MAX_DISTANCE = 128
C_HEADS = 8
FFN_CHUNK = 1024
FFN_ROWS = 1024
PROJ_ROWS = 512
MOD_COLS = 1152
DIFF_ATTN_ROWS = 512
WIN_ATTN_BLOCKS = 4
SUBLANES = 8
GLA_CHUNK = 128
GLA_CHUNKS_PER_STEP = 2
GLA_HEADS_PER_STEP = 2
LOG2E = math.log2(math.e)
V7X_VMEM_LIMIT_BYTES = 56 * 1024 * 1024

_TRANS_B = (((1,), (1,)), ((), ()))
_TRANS_A = (((0,), (0,)), ((), ()))


def _params(*sem):
    return pltpu.CompilerParams(dimension_semantics=sem, vmem_limit_bytes=V7X_VMEM_LIMIT_BYTES)


def _dot(a, b):
    return jnp.dot(a, b, preferred_element_type=F32)


def _silu(t):
    u = 0.5 * t
    return u + u * jnp.tanh(u)


def _norm_mod(x, g, shift, scale):
    ms = jnp.mean(x * x, axis=-1, keepdims=True)
    y = x * lax.rsqrt(ms + EPS) * g
    return y * (1.0 + scale) + shift


def _norm_halves(x, g, first):
    sq = x * x
    s0 = jnp.sum(jnp.where(first, sq, 0.0), axis=-1, keepdims=True)
    s1 = jnp.sum(jnp.where(first, 0.0, sq), axis=-1, keepdims=True)
    ms = jnp.where(first, s0, s1) * (1.0 / HEAD_DIM)
    return x * lax.rsqrt(ms + EPS) * g


def _mod_kernel(c_ref, w_ref, b_ref, o_ref):
    sc = _silu(c_ref[...])
    o_ref[0] = jnp.dot(sc, w_ref[0], precision=lax.Precision.HIGHEST,
                       preferred_element_type=F32) + b_ref[0]


def _mod_call(c, ada_w, ada_b):
    depth, d, m = ada_w.shape
    bsz = c.shape[0]
    tn = MOD_COLS
    assert m % tn == 0
    return pl.pallas_call(
        _mod_kernel,
        grid=(depth, m // tn),
        in_specs=[pl.BlockSpec((bsz, d), lambda l, j: (0, 0)),
                  pl.BlockSpec((1, d, tn), lambda l, j: (l, 0, j)),
                  pl.BlockSpec((1, 1, tn), lambda l, j: (l, 0, j))],
        out_specs=pl.BlockSpec((1, bsz, tn), lambda l, j: (l, 0, j)),
        out_shape=jax.ShapeDtypeStruct((depth, bsz, m), F32),
        compiler_params=_params("arbitrary", "arbitrary"),
        name="adaln_mod",
    )(c, ada_w, ada_b.reshape(depth, 1, m))


def _ffn_kernel(x_ref, g_ref, sh_ref, sc_ref, gate_ref, wu_ref, wd_ref, *rest, chunks):
    o_ref = rest[-1]
    x = x_ref[0]
    if len(rest) > 1:
        n_in = (len(rest) - 2) // 2
        mix = _dot(rest[1][0], rest[1 + n_in][...])
        for y_ref, w_ref in zip(rest[2:1 + n_in], rest[2 + n_in:1 + 2 * n_in]):
            mix += _dot(y_ref[0], w_ref[...])
        x = x + rest[0][0] * mix
    h = _norm_mod(x, g_ref[...], sh_ref[0], sc_ref[0]).astype(BF16)
    f = wd_ref.shape[0]
    acc = None
    for c0, cw in chunks:
        a = _dot(h, wu_ref[:, c0:c0 + cw])
        b = _dot(h, wu_ref[:, f + c0:f + c0 + cw])
        part = _dot((_silu(a) * b).astype(BF16), wd_ref[c0:c0 + cw, :])
        acc = part if acc is None else acc + part
    o_ref[0] = x + 0.5 * gate_ref[0] * acc


def _ffn_call(x, g, shift, scale, gate, w_up, w_down, which, mixer=None):
    bsz, s, d = x.shape
    f = w_down.shape[2]
    tm = min(FFN_ROWS, s)
    chunks = tuple((c0, min(FFN_CHUNK, f - c0)) for c0 in range(0, f, FFN_CHUNK))
    row = lambda b, i: (b, i, 0)
    vec = lambda b, i: (b, 0, 0)
    resident = dict(pipeline_mode=pl.Buffered(1))
    in_specs = [pl.BlockSpec((1, tm, d), row),
                pl.BlockSpec((1, d), lambda b, i: (0, 0)),
                pl.BlockSpec((1, 1, d), vec),
                pl.BlockSpec((1, 1, d), vec),
                pl.BlockSpec((1, 1, d), vec),
                pl.BlockSpec((None, None) + w_up.shape[2:], lambda b, i: which + (0, 0), **resident),
                pl.BlockSpec((None, None) + w_down.shape[2:], lambda b, i: which + (0, 0), **resident)]
    args = [x, g.reshape(1, d), shift, scale, gate, w_up, w_down]
    if mixer is not None:
        mixer_gate, ys, w_out = mixer
        wk = ys[0].shape[-1]
        in_specs += ([pl.BlockSpec((1, 1, d), vec)] + [pl.BlockSpec((1, tm, wk), row)] * len(ys)
                     + [pl.BlockSpec((wk, d), functools.partial(lambda b, i, k: (k, 0), k=k), **resident)
                        for k in range(len(ys))])
        args += [mixer_gate, *ys, *([w_out] * len(ys))]
    return pl.pallas_call(
        functools.partial(_ffn_kernel, chunks=chunks),
        grid=(bsz, s // tm),
        in_specs=in_specs,
        out_specs=pl.BlockSpec((1, tm, d), row),
        out_shape=jax.ShapeDtypeStruct(x.shape, F32),
        compiler_params=_params("arbitrary", "arbitrary"),
        name="ffn",
    )(*args)


def _even_in_kernel(x_ref, g_ref, sh_ref, sc_ref, w_ref, o_ref):
    h = _norm_mod(x_ref[0], g_ref[...], sh_ref[0], sc_ref[0]).astype(BF16)
    o_ref[0] = _dot(h, w_ref[...]).astype(BF16)


def _even_in_call(x, g, shift, scale, w):
    bsz, s, d = x.shape
    n = w.shape[1]
    tm = min(PROJ_ROWS, s)
    row = lambda b, i: (b, i, 0)
    vec = lambda b, i: (b, 0, 0)
    return pl.pallas_call(
        _even_in_kernel,
        grid=(bsz, s // tm),
        in_specs=[pl.BlockSpec((1, tm, d), row),
                  pl.BlockSpec((1, d), lambda b, i: (0, 0)),
                  pl.BlockSpec((1, 1, d), vec),
                  pl.BlockSpec((1, 1, d), vec),
                  pl.BlockSpec((d, n), lambda b, i: (0, 0))],
        out_specs=pl.BlockSpec((1, tm, n), row),
        out_shape=jax.ShapeDtypeStruct((bsz, s, n), BF16),
        compiler_params=_params("arbitrary", "arbitrary"),
        name="even_in_proj",
    )(x, g.reshape(1, d), shift, scale, w)


def _log_forget_and_input_gate(f_raw, clb, layer):
    e = jnp.exp(clb - jnp.max(clb, axis=0, keepdims=True))
    lb = jnp.sum(e[1:layer + 1], axis=0, keepdims=True) / jnp.sum(e, axis=0, keepdims=True)
    half_th = 0.5 * jnp.tanh(0.5 * f_raw)
    f = lb + (1.0 - lb) * (0.5 + half_th)
    lf2 = jnp.where(f > 0.0, jnp.log2(f), (jnp.log(1.0 - lb) + f_raw) * LOG2E)
    return lf2, (1.0 - lb) * (0.5 - half_th)


def _odd_in_kernel(x_ref, g_ref, sh_ref, sc_ref, w_ref, clb_ref,
                   qs_ref, lff_ref, lfb_ref, kf_ref, kb_ref, v_ref, gs_ref, *, layer):
    h = _norm_mod(x_ref[0], g_ref[...], sh_ref[0], sc_ref[0]).astype(BF16)
    hk = qs_ref.shape[-1]
    seg = lambda n: _dot(h, w_ref[:, n * hk:(n + 1) * hk])
    qs_ref[0] = _silu(seg(0)).astype(BF16)
    for n, lf_ref, k_ref in ((1, lff_ref, kf_ref), (2, lfb_ref, kb_ref)):
        lf2, k = _log_forget_and_input_gate(seg(n), clb_ref[n - 1], layer)
        lf_ref[0] = lf2
        k_ref[0] = k.astype(BF16)
    v_ref[0] = seg(3).astype(BF16)
    gs_ref[0] = _silu(seg(4)).astype(BF16)


def _odd_in_call(x, g, shift, scale, w, clb, layer):
    bsz, s, d = x.shape
    hk = clb.shape[-1]
    tm = min(PROJ_ROWS, s)
    row = lambda b, i: (b, i, 0)
    vec = lambda b, i: (b, 0, 0)
    out = lambda dt: jax.ShapeDtypeStruct((bsz, s, hk), dt)
    return pl.pallas_call(
        functools.partial(_odd_in_kernel, layer=layer),
        grid=(bsz, s // tm),
        in_specs=[pl.BlockSpec((1, tm, d), row),
                  pl.BlockSpec((1, d), lambda b, i: (0, 0)),
                  pl.BlockSpec((1, 1, d), vec),
                  pl.BlockSpec((1, 1, d), vec),
                  pl.BlockSpec(w.shape, lambda b, i: (0, 0), pipeline_mode=pl.Buffered(1)),
                  pl.BlockSpec(clb.shape, lambda b, i: (0, 0, 0))],
        out_specs=[pl.BlockSpec((1, tm, hk), row)] * 7,
        out_shape=[out(BF16), out(F32), out(F32), out(BF16), out(BF16), out(BF16), out(BF16)],
        compiler_params=_params("arbitrary", "arbitrary"),
        name="odd_in_proj",
    )(x, g.reshape(1, d), shift, scale, w, clb)


def _t5_bucket(rel):
    half = NUM_BUCKETS // 2
    max_exact = half // 2
    n = jnp.abs(rel)
    nf = jnp.maximum(n, 1).astype(F32)
    large = max_exact + (jnp.log(nf / max_exact) / math.log(MAX_DISTANCE / max_exact)
                         * (half - max_exact)).astype(jnp.int32)
    large = jnp.minimum(large, half - 1)
    return jnp.where(rel > 0, half, 0) + jnp.where(n < max_exact, n, large)


def _bias_tiles(rel_bias):
    r = jnp.arange(LANES)[:, None]
    c = jnp.arange(LANES)[None, :]

    def lookup(rel, table):
        hot = jax.nn.one_hot(_t5_bucket(rel), NUM_BUCKETS, dtype=F32)
        return jnp.einsum('...k,kh->h...', hot, table.astype(F32), precision=lax.Precision.HIGHEST)

    rel_a = (jnp.arange(-2, 3) * LANES)[:, None, None] + (c - r)[None]
    rel_b = jnp.arange(3 * LANES)[None, :] - LANES - r
    return lookup(rel_a, rel_bias[:, :A_HEADS]), lookup(rel_b, rel_bias[:, A_HEADS:])


def _diff_attn_kernel(q_ref, k_ref, v_ref, bias_ref, qg_ref, kg_ref, lam_ref, sg_ref, o_ref, *, lam_init, tq):
    s_len = k_ref.shape[1]
    nb = s_len // LANES
    first = lax.broadcasted_iota(jnp.int32, (1, LANES), 1) < HEAD_DIM
    kn = _norm_halves(k_ref[0].astype(F32), kg_ref[...], first).astype(BF16)
    v = v_ref[0]

    lp = lam_ref[...]
    lam = (jnp.exp(jnp.sum(lp[0:1] * lp[1:2], axis=-1, keepdims=True))
           - jnp.exp(jnp.sum(lp[2:3] * lp[3:4], axis=-1, keepdims=True)) + lam_init)

    def add_bias(sc, first_block):
        rows = []
        for a in range(tq // LANES):
            tiles = []
            for j in range(nb):
                d = j - (first_block + a)
                tile = sc[a * LANES:(a + 1) * LANES, j * LANES:(j + 1) * LANES]
                tiles.append(tile if d <= -2 else tile + bias_ref[0, min(d, 2) + 2])
            rows.append(jnp.concatenate(tiles, axis=1))
        return jnp.concatenate(rows, axis=0)

    for t in range(s_len // tq):
        rows = slice(t * tq, (t + 1) * tq)
        qn = _norm_halves(q_ref[0, rows, :].astype(F32), qg_ref[...], first) * (HEAD_DIM ** -0.5 * LOG2E)
        qms = [jnp.where(first, qn, 0.0).astype(BF16), jnp.where(first, 0.0, qn).astype(BF16)]
        scs = [add_bias(lax.dot_general(qm, kn, _TRANS_B, preferred_element_type=F32), t * (tq // LANES))
               for qm in qms]
        es = [jnp.exp2(sc - jnp.max(sc, axis=-1, keepdims=True)) for sc in scs]
        rs = [1.0 / jnp.sum(e, axis=-1, keepdims=True) for e in es]
        evs = [_dot(e.astype(BF16), v) for e in es]
        o = evs[0] * rs[0] - evs[1] * (lam * rs[1])
        o = o * lax.rsqrt(jnp.mean(o * o, axis=-1, keepdims=True) + EPS) * sg_ref[...] * (1.0 - lam_init)
        o_ref[0, rows, :] = o.astype(BF16)


def _diff_attn_call(qkv, tiles_a, qk_g, lam_p, subln_g, lam_init):
    bsz, s, _ = qkv.shape
    dup = lambda g: jnp.concatenate([g, g]).reshape(1, LANES)
    col = lambda blk: pl.BlockSpec((1, s, LANES), lambda b, h: (b, 0, blk * A_HEADS + h))
    one = pl.BlockSpec((1, LANES), lambda b, h: (0, 0))
    rel_tiles = (tiles_a - tiles_a[:, :1]) * LOG2E
    return pl.pallas_call(
        functools.partial(_diff_attn_kernel, lam_init=lam_init, tq=min(DIFF_ATTN_ROWS, s)),
        grid=(bsz, A_HEADS),
        in_specs=[col(0), col(1), col(2),
                  pl.BlockSpec((1, 5, LANES, LANES), lambda b, h: (h, 0, 0, 0)),
                  one, one, pl.BlockSpec(lam_p.shape, lambda b, h: (0, 0)), one],
        out_specs=col(0),
        out_shape=jax.ShapeDtypeStruct((bsz, s, A_HEADS * LANES), BF16),
        compiler_params=_params("arbitrary", "arbitrary"),
        name="diff_attn",
    )(qkv, qkv, qkv, rel_tiles, dup(qk_g[0]), dup(qk_g[1]), lam_p, subln_g.reshape(1, LANES))


def _win_attn_kernel(q_ref, k_ref, v_ref, bias_ref, qg_ref, kg_ref, sink_ref, o_ref, kn_scr, *, q_blocks):
    i = pl.program_id(1)
    nb = pl.num_programs(1) * q_blocks
    group = B_Q_HEADS // B_KV_HEADS
    first = lax.broadcasted_iota(jnp.int32, (1, LANES), 1) < HEAD_DIM

    @pl.when(i == 0)
    def _():
        kn_scr[...] = _norm_halves(k_ref[0].astype(F32), kg_ref[...], first).astype(BF16)

    r = lax.broadcasted_iota(jnp.int32, (group * LANES, 1), 0) & (LANES - 1)
    c = lax.broadcasted_iota(jnp.int32, (1, 3 * LANES), 1)
    band = jnp.abs(c - LANES - r) <= WINDOW
    sinks = [jnp.max(sink_ref[kv], axis=-1, keepdims=True) for kv in range(B_KV_HEADS)]

    for u in range(q_blocks):
        blk = i * q_blocks + u
        rows = slice(u * LANES, (u + 1) * LANES)
        starts = [pl.multiple_of(j * LANES, LANES)
                  for j in (jnp.maximum(blk - 1, 0), blk, jnp.minimum(blk + 1, nb - 1))]
        k3 = jnp.concatenate([kn_scr[pl.ds(st, LANES), :] for st in starts], axis=0)
        v3 = jnp.concatenate([v_ref[0, pl.ds(st, LANES), :] for st in starts], axis=0)
        valid = band & ((c >= LANES) | (blk > 0)) & ((c < 2 * LANES) | (blk < nb - 1))

        pairs = []
        for pb in range(B_Q_HEADS // 2):
            qp = _norm_halves(q_ref[0, rows, pb * LANES:(pb + 1) * LANES].astype(F32), qg_ref[...], first)
            qp = qp * (HEAD_DIM ** -0.5 * LOG2E)
            pairs.append((qp, pltpu.roll(qp, HEAD_DIM, 1)))

        for kv in range(B_KV_HEADS):
            kv_lanes = first if kv == 0 else jnp.logical_not(first)
            pbs = range(kv * group // 2, (kv + 1) * group // 2)
            qs = jnp.concatenate([jnp.where(kv_lanes, pairs[pb][0 if a == kv else 1], 0.0).astype(BF16)
                                  for pb in pbs for a in range(2)], axis=0)
            sc = lax.dot_general(qs, k3, _TRANS_B, preferred_element_type=F32) + bias_ref[kv]
            sc = jnp.where(valid, sc, -jnp.inf)
            sk = sinks[kv]
            m = jnp.maximum(jnp.max(sc, axis=-1, keepdims=True), sk)
            e = jnp.exp2(sc - m)
            p = e * (1.0 / (jnp.sum(e, axis=-1, keepdims=True) + jnp.exp2(sk - m)))
            o = _dot(p.astype(BF16), v3)
            for n, pb in enumerate(pbs):
                halves = []
                for a in range(2):
                    oh = o[(2 * n + a) * LANES:(2 * n + a + 1) * LANES]
                    halves.append(oh if a == kv else pltpu.roll(oh, HEAD_DIM, 1))
                o_ref[0, rows, pb * LANES:(pb + 1) * LANES] = jnp.where(first, halves[0], halves[1]).astype(BF16)


def _win_attn_call(qkv, tiles_b, qk_g, sink):
    bsz, s, n = qkv.shape
    qw = B_Q_HEADS * HEAD_DIM
    group = B_Q_HEADS // B_KV_HEADS
    q_blocks = min(WIN_ATTN_BLOCKS, s // LANES)
    tq = q_blocks * LANES
    q_blk = (2 * A_HEADS * LANES + A_HEADS * LANES) // qw
    k_blk = (2 * A_HEADS * LANES + A_HEADS * LANES + qw) // LANES
    dup = lambda g: jnp.concatenate([g, g]).reshape(1, LANES)
    bias = tiles_b.reshape(B_KV_HEADS, group * LANES, 3 * LANES) * LOG2E
    sink_rows = jnp.broadcast_to(jnp.repeat(sink.astype(F32).reshape(B_KV_HEADS, group) * LOG2E, LANES, axis=1)[..., None],
                                 (B_KV_HEADS, group * LANES, LANES))
    return pl.pallas_call(
        functools.partial(_win_attn_kernel, q_blocks=q_blocks),
        grid=(bsz, s // tq),
        in_specs=[pl.BlockSpec((1, tq, qw), lambda b, i: (b, i, q_blk)),
                  pl.BlockSpec((1, s, LANES), lambda b, i: (b, 0, k_blk)),
                  pl.BlockSpec((1, s, LANES), lambda b, i: (b, 0, k_blk + 1)),
                  pl.BlockSpec(bias.shape, lambda b, i: (0, 0, 0)),
                  pl.BlockSpec((1, LANES), lambda b, i: (0, 0)),
                  pl.BlockSpec((1, LANES), lambda b, i: (0, 0)),
                  pl.BlockSpec(sink_rows.shape, lambda b, i: (0, 0, 0))],
        out_specs=pl.BlockSpec((1, tq, qw), lambda b, i: (b, i, 0)),
        out_shape=jax.ShapeDtypeStruct((bsz, s, qw), BF16),
        scratch_shapes=[pltpu.VMEM((s, LANES), BF16)],
        compiler_params=_params("arbitrary", "arbitrary"),
        name="win_attn",
    )(qkv, qkv, qkv, bias, dup(qk_g[2]), dup(qk_g[3]), sink_rows)


def _gla_constants(cs):
    t = np.arange(cs)[:, None]
    u = np.arange(cs)[None, :]
    sums, pairs = [], []
    for rev in (False, True):
        mats = [(u >= t) if rev else (u <= t)]
        for h in (2, 4):
            start = (t // (2 * h)) * 2 * h
            later = (t // h) % 2 == 1
            if rev:
                b = start + h
                mats.append(np.where(later, (u >= b) & (u < t), (u >= t) & (u < b)))
            else:
                b = start + h - 1
                mats.append(np.where(later, (u > b) & (u <= t), (u > t) & (u <= b)))
        sums.append(np.concatenate(mats, axis=0))
        lv, h = [], 1
        while h < cs:
            same = (t // (2 * h)) == (u // (2 * h))
            t_later = (t // h) % 2 == 1
            u_later = (u // h) % 2 == 1
            lv.append(same & (t_later != rev) & (u_later == rev))
            h *= 2
        pairs.append(np.stack(lv))
    return jnp.asarray(np.stack(sums), BF16), jnp.asarray(np.stack(pairs), F32)


def _gla_chunks(chains, pairs_ref):
    cs = chains[0][0].shape[0]
    odd_row = (lax.broadcasted_iota(jnp.int32, (cs, 1), 0) & 1) == 1
    ex_alls = []
    for q, k, v, lf2, st, sums, rev in chains:
        hi = lf2.astype(BF16)
        lo = (lf2 - hi.astype(F32)).astype(BF16)
        ex_alls.append(_dot(sums, hi) + _dot(sums, lo))

    outs, states, gcs = [], [], []
    for (q, k, v, lf2, st, sums, rev), ex_all in zip(chains, ex_alls):
        if isinstance(st, int):
            st = states[st]
        gc = ex_all[0:cs]
        tot = gc[0:1] if rev else gc[cs - 1:cs]
        o = lax.dot_general((q * jnp.exp2(gc)).astype(BF16), st.astype(BF16), _TRANS_B, preferred_element_type=F32)
        kd = (k * jnp.exp2(tot - gc)).astype(BF16)
        states.append(jnp.exp2(tot) * st + lax.dot_general(v, kd, _TRANS_A, preferred_element_type=F32))
        outs.append(o + jnp.sum(q * k, axis=-1, keepdims=True) * v.astype(F32))
        gcs.append(gc)

    attns = [jnp.zeros((cs, cs), F32) for _ in chains]
    for level in range(cs.bit_length() - 1):
        h = 1 << level
        for n, ((q, k, v, lf2, st, sums, rev), ex_all, gc) in enumerate(zip(chains, ex_alls, gcs)):
            if h == 1:
                ex = jnp.where(odd_row != rev, lf2, 0.0)
            elif h < SUBLANES:
                ex = ex_all[level * cs:(level + 1) * cs]
            else:
                bounds = [p * 2 * h + (h if rev else h - 1) for p in range(cs // (2 * h))]
                ref = jnp.concatenate([jnp.broadcast_to(gc[b:b + 1], (2 * h, gc.shape[1])) for b in bounds], axis=0)
                d = gc - ref
                ex = jnp.minimum(d, -d)
            x = jnp.exp2(ex)
            pr = lax.dot_general((q * x).astype(BF16), (k * x).astype(BF16), _TRANS_B, preferred_element_type=F32)
            attns[n] = attns[n] + pr * pairs_ref[int(rev), level]
    return [(o + _dot(attn.astype(BF16), ch[2]), st) for o, attn, ch, st in zip(outs, attns, chains, states)]


def _gla_kernel(qs_ref, lff_ref, lfb_ref, kf_ref, kb_ref, v_ref, gs_ref, og_ref, sums_ref, pairs_ref, y_ref,
                of_scr, ob_scr):
    s_len = qs_ref.shape[1]
    heads = qs_ref.shape[2] // LANES
    cs = sums_ref.shape[2]
    nc = s_len // cs
    specs = [(j, rev, lf_ref, k_ref, o_scr, other_scr) for j in range(heads)
             for rev, lf_ref, k_ref, o_scr, other_scr in ((False, lff_ref, kf_ref, of_scr, ob_scr),
                                                          (True, lfb_ref, kb_ref, ob_scr, of_scr))]
    n = len(specs)

    def finish(o, r, l):
        y = o * lax.rsqrt(jnp.mean(o * o, axis=-1, keepdims=True) + EPS) * og_ref[...]
        y_ref[0, r, l] = (y * gs_ref[0, r, l].astype(F32)).astype(BF16)

    states = (jnp.zeros((LANES, LANES), F32),) * n
    for c in range(nc // GLA_CHUNKS_PER_STEP):
        where, chains = [], []
        for step in range(GLA_CHUNKS_PER_STEP):
            order = c * GLA_CHUNKS_PER_STEP + step
            for m, (j, rev, lf_ref, k_ref, _, _) in enumerate(specs):
                blk = (nc - 1 - order) if rev else order
                r = slice(blk * cs, (blk + 1) * cs)
                l = slice(j * LANES, (j + 1) * LANES)
                where.append((r, l, order))
                chains.append((qs_ref[0, r, l].astype(F32), k_ref[0, r, l].astype(F32), v_ref[0, r, l],
                               lf_ref[0, r, l], states[m] if step == 0 else (step - 1) * n + m,
                               sums_ref[int(rev)], rev))
        outs = _gla_chunks(chains, pairs_ref)
        for (r, l, order), (o, _), (*_, o_scr, other_scr) in zip(where, outs, specs * GLA_CHUNKS_PER_STEP):
            if 2 * order >= nc:
                finish(o + other_scr[r, l], r, l)
            else:
                o_scr[r, l] = o
        states = tuple(st for _, st in outs[-n:])


def _gla_call(qs, lff, lfb, kf, kb, v, gs, out_g):
    bsz, s, hk = qs.shape
    sums, pairs = _gla_constants(min(GLA_CHUNK, s))
    width = GLA_HEADS_PER_STEP * LANES
    blk = pl.BlockSpec((1, s, width), lambda b, h: (b, 0, h))
    return pl.pallas_call(
        _gla_kernel,
        grid=(bsz, hk // width),
        in_specs=[blk] * 7 + [pl.BlockSpec((1, LANES), lambda b, h: (0, 0)),
                              pl.BlockSpec(sums.shape, lambda b, h: (0, 0, 0)),
                              pl.BlockSpec(pairs.shape, lambda b, h: (0, 0, 0, 0))],
        out_specs=blk,
        out_shape=jax.ShapeDtypeStruct((bsz, s, hk), BF16),
        scratch_shapes=[pltpu.VMEM((s, width), F32), pltpu.VMEM((s, width), F32)],
        compiler_params=_params("arbitrary", "arbitrary"),
        name="gla",
    )(qs, lff, lfb, kf, kb, v, gs, out_g.reshape(1, LANES), sums, pairs)


def kernel(x, c, ada_w, ada_b, norm_g, ffn_up, ffn_down, even_w_in, even_w_out, qk_norm_g, diff_lambda,
           diff_subln_g, sink_logit, rel_bias, odd_w_in, odd_w_out, c_lower_bound, c_out_norm_g):
    bsz, s, d = x.shape
    depth = ada_w.shape[0]
    assert d == 2 * A_HEADS * LANES == B_Q_HEADS * LANES == C_HEADS * LANES
    assert s % (GLA_CHUNK * GLA_CHUNKS_PER_STEP) == 0 and s % min(DIFF_ATTN_ROWS, s) == 0

    mod_all = _mod_call(c, ada_w, ada_b)
    tiles_a, tiles_b = _bias_tiles(rel_bias)
    w_up, w_down = ffn_up.astype(BF16), ffn_down.astype(BF16)
    for l in range(depth):
        mod = mod_all[l].reshape(bsz, N_MOD, 1, d)
        sh1, sc1, g1, sh2, sc2, g2, sh3, sc3, g3 = (mod[:, k] for k in range(N_MOD))
        x = _ffn_call(x, norm_g[l, 0], sh1, sc1, g1, w_up, w_down, (l, 0))
        if l % 2 == 0:
            e = l // 2
            lam_init = 0.8 - 0.6 * math.exp(-0.3 * l)
            qkv = _even_in_call(x, norm_g[l, 1], sh2, sc2, even_w_in[e].astype(BF16))
            ya = _diff_attn_call(qkv, tiles_a, qk_norm_g[e], diff_lambda[e].astype(F32), diff_subln_g[e], lam_init)
            yb = _win_attn_call(qkv, tiles_b, qk_norm_g[e], sink_logit[e])
            mixer = (g2, [ya, yb], even_w_out[e].astype(BF16))
        else:
            o = l // 2
            parts = _odd_in_call(x, norm_g[l, 1], sh2, sc2, odd_w_in[o].astype(BF16),
                                 c_lower_bound.astype(F32), l)
            mixer = (g2, [_gla_call(*parts, c_out_norm_g[o])], odd_w_out[o].astype(BF16))
        x = _ffn_call(x, norm_g[l, 2], sh3, sc3, g3, w_up, w_down, (l, 1), mixer)
    return x
```
